```python
import math
import jax, jax.numpy as jnp
from jax import lax
import numpy as np

D_MODEL = 2048
BATCH = 2
SEQ = 16384
DEPTH = 2

CHUNK = 64
QBLK = 128
PLE_DIM = 256
ATT_HEADS = 8
ATT_DK = 64
ATT_DV = 2 * ATT_DK
ATT_WIDTH = ATT_HEADS * ATT_DV
SSM_WIDTH = 1024
SSM_GROUP = 16
SSM_GROUPS = SSM_WIDTH // SSM_GROUP
SSM_STATE = 64
SPLIT_SIZES = (
    ATT_HEADS * 2 * ATT_DK,
    ATT_HEADS * 2 * ATT_DK,
    ATT_WIDTH,
    ATT_WIDTH,
    SSM_WIDTH,
    SSM_WIDTH,
    D_MODEL,
    D_MODEL,
    D_MODEL,
)
IN_COLS = sum(SPLIT_SIZES)
SPLIT_POINTS = tuple(int(v) for v in np.cumsum(SPLIT_SIZES)[:-1])
ALPHA = (2.0 * DEPTH) ** 0.25
BETA = (8.0 * DEPTH) ** -0.25
LN_EPS = 1e-5
RMS_EPS = 1e-5
NEG_INF = -1e30

kernel_name = "hybrid_diffattn_s5_deepnorm_trunk"


def layer_norm(x, g, b):
    xf = x.astype(jnp.float32)
    mu = jnp.mean(xf, axis=-1, keepdims=True)
    xc = xf - mu
    var = jnp.mean(xc * xc, axis=-1, keepdims=True)
    y = xc * lax.rsqrt(var + LN_EPS) * g.astype(jnp.float32) + b.astype(jnp.float32)
    return y.astype(x.dtype)


def diff_attention(q, k, v, lam, subln_w, lambda_init):
    B, S = q.shape[0], q.shape[1]
    nblk = S // QBLK
    q1 = q[:, :, :, 0, :].transpose(0, 2, 1, 3)
    q2 = q[:, :, :, 1, :].transpose(0, 2, 1, 3)
    k1 = k[:, :, :, 0, :].transpose(0, 2, 1, 3)
    k2 = k[:, :, :, 1, :].transpose(0, 2, 1, 3)
    vh = v.transpose(0, 2, 1, 3)
    to_blocks = lambda t: t.reshape(B, ATT_HEADS, nblk, QBLK, ATT_DK).transpose(2, 0, 1, 3, 4)
    q1b, q2b = to_blocks(q1), to_blocks(q2)
    kpos = jnp.arange(S)
    slopes = 2.0 ** (-8.0 * (jnp.arange(ATT_HEADS, dtype=jnp.float32) + 1.0) / ATT_HEADS)
    scale = 1.0 / math.sqrt(ATT_DK)

    def attend_block(args):
        qb1, qb2, blk = args
        qpos = blk * QBLK + jnp.arange(QBLK)
        allowed = (kpos // CHUNK)[None, :] <= (qpos // CHUNK)[:, None]
        dist = jnp.abs(qpos[:, None] - kpos[None, :]).astype(jnp.float32)
        bias = -slopes[:, None, None] * dist[None]

        def probs(qb, kk):
            s = jnp.einsum('bhqd,bhkd->bhqk', qb, kk).astype(jnp.float32) * scale + bias
            s = jnp.where(allowed, s, NEG_INF)
            return jax.nn.softmax(s, axis=-1)

        a = probs(qb1, k1) - lam * probs(qb2, k2)
        return jnp.einsum('bhqk,bhkd->bhqd', a.astype(vh.dtype), vh)

    out = lax.map(attend_block, (q1b, q2b, jnp.arange(nblk)))
    out = out.transpose(1, 0, 3, 2, 4).reshape(B, S, ATT_HEADS, ATT_DV)
    of = out.astype(jnp.float32)
    of = of * lax.rsqrt(jnp.mean(of * of, axis=-1, keepdims=True) + RMS_EPS)
    of = of * subln_w.astype(jnp.float32) * (1.0 - lambda_init)
    return of.reshape(B, S, ATT_WIDTH).astype(v.dtype)


def s5_branch(u, a_re, a_im, log_dt, b_re, b_im, c_re, c_im, d_skip, w_glu):
    B, S = u.shape[0], u.shape[1]
    f32 = jnp.float32
    ar, ai = a_re.astype(f32), a_im.astype(f32)
    dt = jnp.exp(log_dt.astype(f32))[:, None]
    mag = jnp.exp(ar * dt)
    ph = ai * dt
    abar_re, abar_im = mag * jnp.cos(ph), mag * jnp.sin(ph)
    nr, ni = abar_re - 1.0, abar_im
    den = ar * ar + ai * ai
    fr = (nr * ar + ni * ai) / den
    fi = (ni * ar - nr * ai) / den
    br, bi = b_re.astype(f32), b_im.astype(f32)
    bbar_re = fr[..., None] * br - fi[..., None] * bi
    bbar_im = fr[..., None] * bi + fi[..., None] * br
    ug = u.astype(f32).reshape(B, S, SSM_GROUPS, SSM_GROUP)
    bu_re = jnp.einsum('bsgh,gph->bsgp', ug, bbar_re)
    bu_im = jnp.einsum('bsgh,gph->bsgp', ug, bbar_im)
    a_re_s = jnp.broadcast_to(abar_re[None, None], (1, S, SSM_GROUPS, SSM_STATE))
    a_im_s = jnp.broadcast_to(abar_im[None, None], (1, S, SSM_GROUPS, SSM_STATE))

    def combine(e1, e2):
        ar1, ai1, br1, bi1 = e1
        ar2, ai2, br2, bi2 = e2
        return (ar2 * ar1 - ai2 * ai1,
                ar2 * ai1 + ai2 * ar1,
                ar2 * br1 - ai2 * bi1 + br2,
                ar2 * bi1 + ai2 * br1 + bi2)

    _, _, xr, xi = lax.associative_scan(combine, (a_re_s, a_im_s, bu_re, bu_im), axis=1)
    y = (jnp.einsum('bsgp,ghp->bsgh', xr, c_re.astype(f32))
         - jnp.einsum('bsgp,ghp->bsgh', xi, c_im.astype(f32)))
    y = y.reshape(B, S, SSM_WIDTH) + d_skip.astype(f32) * u.astype(f32)
    y = jax.nn.gelu(y)
    y = y * jax.nn.sigmoid(y @ w_glu.astype(f32))
    return y.astype(u.dtype)


def setup_inputs(seed: int = 0) -> dict:
    key = jax.random.key(seed)
    ks = jax.random.split(key, 24)
    f32 = jnp.float32
    nrm = lambda k, shape, s: jax.random.normal(k, shape, f32) * s
    x = jax.random.normal(ks[0], (BATCH, SEQ, D_MODEL), f32)
    p = jax.random.normal(ks[1], (DEPTH, BATCH, SEQ, PLE_DIM), f32)
    w_in = nrm(ks[2], (DEPTH, D_MODEL, IN_COLS), D_MODEL ** -0.5)
    w_att_out = nrm(ks[3], (DEPTH, ATT_WIDTH, D_MODEL), BETA * ATT_WIDTH ** -0.5)
    w_ssm_out = nrm(ks[4], (DEPTH, SSM_WIDTH, D_MODEL), BETA * SSM_WIDTH ** -0.5)
    w_out = nrm(ks[5], (DEPTH, D_MODEL, D_MODEL), BETA * D_MODEL ** -0.5)
    w_ple = nrm(ks[6], (DEPTH, PLE_DIM, D_MODEL), PLE_DIM ** -0.5)
    lambda_q1 = nrm(ks[7], (DEPTH, ATT_DK), 0.1)
    lambda_k1 = nrm(ks[8], (DEPTH, ATT_DK), 0.1)
    lambda_q2 = nrm(ks[9], (DEPTH, ATT_DK), 0.1)
    lambda_k2 = nrm(ks[10], (DEPTH, ATT_DK), 0.1)
    subln_w = 1.0 + nrm(ks[11], (DEPTH, ATT_DV), 0.02)
    ssm_a_re = -0.5 + nrm(ks[12], (DEPTH, SSM_GROUPS, SSM_STATE), 0.01)
    ssm_a_im = (math.pi * jnp.arange(SSM_STATE, dtype=f32))[None, None, :] + nrm(ks[13], (DEPTH, SSM_GROUPS, SSM_STATE), 0.01)
    ssm_log_dt = jax.random.uniform(ks[14], (DEPTH, SSM_GROUPS), f32, math.log(1e-3), math.log(1e-1))
    ssm_b_re = nrm(ks[15], (DEPTH, SSM_GROUPS, SSM_STATE, SSM_GROUP), (2.0 * SSM_GROUP) ** -0.5)
    ssm_b_im = nrm(ks[16], (DEPTH, SSM_GROUPS, SSM_STATE, SSM_GROUP), (2.0 * SSM_GROUP) ** -0.5)
    ssm_c_re = nrm(ks[17], (DEPTH, SSM_GROUPS, SSM_GROUP, SSM_STATE), SSM_STATE ** -0.5)
    ssm_c_im = nrm(ks[18], (DEPTH, SSM_GROUPS, SSM_GROUP, SSM_STATE), SSM_STATE ** -0.5)
    ssm_d = nrm(ks[19], (DEPTH, SSM_WIDTH), 1.0)
    ssm_w_glu = nrm(ks[20], (DEPTH, SSM_WIDTH, SSM_WIDTH), SSM_WIDTH ** -0.5)
    ln_g = 1.0 + nrm(ks[21], (DEPTH, D_MODEL), 0.02)
    ln_b = nrm(ks[22], (DEPTH, D_MODEL), 0.02)
    return {"x": x, "p": p, "w_in": w_in, "w_att_out": w_att_out, "w_ssm_out": w_ssm_out,
            "w_out": w_out, "w_ple": w_ple, "lambda_q1": lambda_q1, "lambda_k1": lambda_k1,
            "lambda_q2": lambda_q2, "lambda_k2": lambda_k2, "subln_w": subln_w,
            "ssm_a_re": ssm_a_re, "ssm_a_im": ssm_a_im, "ssm_log_dt": ssm_log_dt,
            "ssm_b_re": ssm_b_re, "ssm_b_im": ssm_b_im, "ssm_c_re": ssm_c_re,
            "ssm_c_im": ssm_c_im, "ssm_d": ssm_d, "ssm_w_glu": ssm_w_glu,
            "ln_g": ln_g, "ln_b": ln_b}


def reference(x, p, w_in, w_att_out, w_ssm_out, w_out, w_ple, lambda_q1, lambda_k1,
              lambda_q2, lambda_k2, subln_w, ssm_a_re, ssm_a_im, ssm_log_dt,
              ssm_b_re, ssm_b_im, ssm_c_re, ssm_c_im, ssm_d, ssm_w_glu, ln_g, ln_b):
    B, S = x.shape[0], x.shape[1]
    for i in range(DEPTH):
        lambda_init = 0.8 - 0.6 * math.exp(-0.3 * i)
        proj = x @ w_in[i]
        q, k, v, z_att, u, z_ssm, g_att, g_ssm, g_ple = jnp.split(proj, SPLIT_POINTS, axis=-1)
        q = q.reshape(B, S, ATT_HEADS, 2, ATT_DK)
        k = k.reshape(B, S, ATT_HEADS, 2, ATT_DK)
        v = v.reshape(B, S, ATT_HEADS, ATT_DV)
        lam = (jnp.exp(jnp.sum(lambda_q1[i].astype(jnp.float32) * lambda_k1[i].astype(jnp.float32)))
               - jnp.exp(jnp.sum(lambda_q2[i].astype(jnp.float32) * lambda_k2[i].astype(jnp.float32)))
               + lambda_init)
        y_att = diff_attention(q, k, v, lam, subln_w[i], lambda_init) * jax.nn.silu(z_att)
        y_ssm = s5_branch(u, ssm_a_re[i], ssm_a_im[i], ssm_log_dt[i], ssm_b_re[i], ssm_b_im[i],
                          ssm_c_re[i], ssm_c_im[i], ssm_d[i], ssm_w_glu[i]) * jax.nn.silu(z_ssm)
        merged = (jax.nn.sigmoid(g_att) * (y_att @ w_att_out[i])
                  + jax.nn.sigmoid(g_ssm) * (y_ssm @ w_ssm_out[i]))
        mixer_out = merged @ w_out[i]
        ple = jax.nn.sigmoid(g_ple) * (p[i] @ w_ple[i])
        x = layer_norm(ALPHA * x + mixer_out + ple, ln_g[i], ln_b[i])
    return x
```

```python
import functools
import math

import jax
import jax.numpy as jnp
from jax import lax
from jax.experimental import pallas as pl
from jax.experimental.pallas import tpu as pltpu

F32 = jnp.float32
BF16 = jnp.bfloat16

DEPTH = 2
PLE_DIM = 256
CHUNK = 64
HEADS = 8
DK = 64
DV = 128
ATT_WIDTH = HEADS * DV
SSM_WIDTH = 1024
SSM_GROUP = 16
SSM_GROUPS = SSM_WIDTH // SSM_GROUP
SSM_STATE = 64
ALPHA = (2.0 * DEPTH) ** 0.25
LN_EPS = 1e-5
RMS_EPS = 1e-5
NEG_INF = -1e30

COL_Q, COL_K, COL_V, COL_ZATT, COL_U, COL_ZSSM = 0, 1024, 2048, 3072, 4096, 5120
COL_GATT, COL_GSSM, COL_GPLE = 6144, 8192, 10240
IN_COLS = 12288

ATT_TILE = 256
SSM_T = 16
SSM_PAIRS = SSM_GROUPS // 2
SSM_PAIR_BLOCK = 8
VMEM_LIMIT = 48 * 1024 * 1024


def _cparams(sem):
    return pltpu.CompilerParams(dimension_semantics=sem, vmem_limit_bytes=VMEM_LIMIT)


def _proj_kernel(x_ref, w_ref, o_ref):
    o_ref[...] = jnp.dot(x_ref[...].astype(BF16), w_ref[...],
                         preferred_element_type=F32).astype(o_ref.dtype)


def _proj(x2, w, tm, tn):
    n, d = x2.shape
    c = w.shape[1]
    return pl.pallas_call(
        _proj_kernel,
        grid=(n // tm, c // tn),
        in_specs=[pl.BlockSpec((tm, d), lambda i, j: (i, 0)),
                  pl.BlockSpec((d, tn), lambda i, j: (0, j))],
        out_specs=pl.BlockSpec((tm, tn), lambda i, j: (i, j)),
        out_shape=jax.ShapeDtypeStruct((n, c), F32),
        compiler_params=_cparams(("parallel", "arbitrary")),
        name="in_proj",
    )(x2, w)


def _attn_kernel(slope_ref, lam_ref, qT_ref, ka_ref, vT_ref, w_ref, o_ref,
                 m_scr, l_scr, acc_scr, *, tile, out_scale):
    h = pl.program_id(1)
    qi = pl.program_id(2)
    slope = slope_ref[h]
    lam = lam_ref[0]

    m_scr[...] = jnp.full(m_scr.shape, NEG_INF, F32)
    l_scr[...] = jnp.zeros(l_scr.shape, F32)
    acc_scr[...] = jnp.zeros(acc_scr.shape, F32)

    def update(mp, u, shift, j):
        cm = jnp.max(u, axis=0, keepdims=True) + shift
        m_old = m_scr[mp]
        m_new = jnp.maximum(m_old, cm)
        a = jnp.exp(m_old - m_new)
        p = jnp.exp(u - (m_new - shift))
        l_scr[mp] = a * l_scr[mp] + jnp.sum(p, axis=0, keepdims=True)
        pv = jnp.dot(vT_ref[0, 0, j], p.astype(BF16), preferred_element_type=F32)
        acc_scr[mp] = a * acc_scr[mp] + pv
        m_scr[mp] = m_new

    def past_tile(j, carry):
        shift = slope * ((j - qi) * tile).astype(F32)
        for mp in range(2):
            u = jnp.dot(ka_ref[0, 0, mp, j], qT_ref[0, 0, mp],
                        preferred_element_type=F32)
            update(mp, u, shift, j)
        return carry

    lax.fori_loop(0, qi, past_tile, 0)

    ik = lax.broadcasted_iota(jnp.int32, (tile, tile), 0)
    iq = lax.broadcasted_iota(jnp.int32, (tile, tile), 1)
    allowed = (ik // CHUNK) <= (iq // CHUNK)
    fix = (-2.0 * slope) * jnp.maximum(ik - iq, 0).astype(F32)
    for mp in range(2):
        u = jnp.dot(ka_ref[0, 0, mp, qi], qT_ref[0, 0, mp],
                    preferred_element_type=F32)
        u = jnp.where(allowed, u + fix, NEG_INF)
        update(mp, u, jnp.float32(0.0), qi)

    o = acc_scr[0] / l_scr[0] - lam * (acc_scr[1] / l_scr[1])
    ms = jnp.mean(o * o, axis=0, keepdims=True)
    o = o * lax.rsqrt(ms + RMS_EPS)
    o = o * w_ref[...] * out_scale
    o_ref[0] = o.T


def _attention(slopes, lam, qT, ka, vT, subln_col, lambda_init):
    b, h, _, _, s = qT.shape
    t = ATT_TILE
    nt = s // t
    kern = functools.partial(_attn_kernel, tile=t, out_scale=1.0 - lambda_init)
    return pl.pallas_call(
        kern,
        grid=(b, h, nt),
        in_specs=[
            pl.BlockSpec(memory_space=pltpu.SMEM),
            pl.BlockSpec(memory_space=pltpu.SMEM),
            pl.BlockSpec((1, 1, 2, 128, t), lambda bi, hi, qi: (bi, hi, 0, 0, qi)),
            pl.BlockSpec((1, 1, 2, nt, t, 128), lambda bi, hi, qi: (bi, hi, 0, 0, 0, 0)),
            pl.BlockSpec((1, 1, nt, DV, t), lambda bi, hi, qi: (bi, hi, 0, 0, 0)),
            pl.BlockSpec((DV, 1), lambda bi, hi, qi: (0, 0)),
        ],
        out_specs=pl.BlockSpec((1, t, DV), lambda bi, hi, qi: (bi, qi, hi)),
        out_shape=jax.ShapeDtypeStruct((b, s, ATT_WIDTH), F32),
        scratch_shapes=[pltpu.VMEM((2, 1, t), F32),
                        pltpu.VMEM((2, 1, t), F32),
                        pltpu.VMEM((2, DV, t), F32)],
        compiler_params=_cparams(("parallel", "parallel", "arbitrary")),
        name="diff_attn",
    )(slopes, lam, qT, ka, vT, subln_col)


def _ssm_kernel(u_ref, toep_ref, qre_ref, qim_ref, pre_ref, pim_ref, are_ref, aim_ref,
                y_ref, zre_scr, zim_scr, xre_scr, xim_scr, st_re, st_im, *, ncb):
    npb = SSM_PAIR_BLOCK
    half = SSM_T * SSM_GROUP

    @pl.when(pl.program_id(2) == 0)
    def _():
        st_re[...] = jnp.zeros(st_re.shape, F32)
        st_im[...] = jnp.zeros(st_im.shape, F32)

    for pi in range(npb):
        up = u_ref[pi, 0]
        zre_scr[pl.ds(pi, ncb, stride=npb), :] = jnp.dot(
            up, qre_ref[pi], preferred_element_type=F32)
        zim_scr[pl.ds(pi, ncb, stride=npb), :] = jnp.dot(
            up, qim_ref[pi], preferred_element_type=F32)

    ar = are_ref[...]
    ai = aim_ref[...]

    def step(c, carry):
        xr, xi = carry
        row = pl.multiple_of(c * npb, npb)
        xre_scr[pl.ds(row, npb), :] = xr
        xim_scr[pl.ds(row, npb), :] = xi
        nxr = ar * xr - ai * xi + zre_scr[pl.ds(row, npb), :]
        nxi = ar * xi + ai * xr + zim_scr[pl.ds(row, npb), :]
        return nxr, nxi

    xr, xi = lax.fori_loop(0, ncb, step, (st_re[...], st_im[...]))
    st_re[...] = xr
    st_im[...] = xi

    for pi in range(npb):
        up = u_ref[pi, 0]
        xrp = xre_scr[pl.ds(pi, ncb, stride=npb), :].astype(BF16)
        xip = xim_scr[pl.ds(pi, ncb, stride=npb), :].astype(BF16)
        yc = (jnp.dot(xrp, pre_ref[pi], preferred_element_type=F32)
              + jnp.dot(xip, pim_ref[pi], preferred_element_type=F32))
        y0 = jnp.dot(up[:, :half], toep_ref[2 * pi], preferred_element_type=F32)
        y1 = jnp.dot(up[:, half:], toep_ref[2 * pi + 1], preferred_element_type=F32)
        y_ref[pi, 0, :, :half] = yc[:, :half] + y0
        y_ref[pi, 0, :, half:] = yc[:, half:] + y1


def _ssm_scan(up, toep, qre, qim, pre, pim, are, aim, ncb):
    pairs, b, nc, w = up.shape
    npb = SSM_PAIR_BLOCK
    kern = functools.partial(_ssm_kernel, ncb=ncb)
    return pl.pallas_call(
        kern,
        grid=(b, pairs // npb, nc // ncb),
        in_specs=[
            pl.BlockSpec((npb, 1, ncb, w), lambda bi, gi, ti: (gi, bi, ti, 0)),
            pl.BlockSpec((2 * npb, w // 2, w // 2), lambda bi, gi, ti: (gi, 0, 0)),
            pl.BlockSpec((npb, w, 128), lambda bi, gi, ti: (gi, 0, 0)),
            pl.BlockSpec((npb, w, 128), lambda bi, gi, ti: (gi, 0, 0)),
            pl.BlockSpec((npb, 128, w), lambda bi, gi, ti: (gi, 0, 0)),
            pl.BlockSpec((npb, 128, w), lambda bi, gi, ti: (gi, 0, 0)),
            pl.BlockSpec((npb, 128), lambda bi, gi, ti: (gi, 0)),
            pl.BlockSpec((npb, 128), lambda bi, gi, ti: (gi, 0)),
        ],
        out_specs=pl.BlockSpec((npb, 1, ncb, w), lambda bi, gi, ti: (gi, bi, ti, 0)),
        out_shape=jax.ShapeDtypeStruct((pairs, b, nc, w), F32),
        scratch_shapes=[pltpu.VMEM((ncb * npb, 128), F32) for _ in range(4)]
        + [pltpu.VMEM((npb, 128), F32), pltpu.VMEM((npb, 128), F32)],
        compiler_params=_cparams(("parallel", "parallel", "arbitrary")),
        name="s5_scan",
    )(up, toep, qre, qim, pre, pim, are, aim)


def _ssm_matrices(a_re, a_im, log_dt, b_re, b_im, c_re, c_im):
    hp = lax.Precision.HIGHEST
    g, p, t = SSM_GROUPS, SSM_STATE, SSM_T
    ar, ai = a_re.astype(F32), a_im.astype(F32)
    dt = jnp.exp(log_dt.astype(F32))[:, None]
    mag = jnp.exp(ar * dt)
    ph = ai * dt
    abr, abi = mag * jnp.cos(ph), mag * jnp.sin(ph)
    nr, ni = abr - 1.0, abi
    den = ar * ar + ai * ai
    fr = (nr * ar + ni * ai) / den
    fi = (ni * ar - nr * ai) / den
    br, bi = b_re.astype(F32), b_im.astype(F32)
    bbr = fr[..., None] * br - fi[..., None] * bi
    bbi = fr[..., None] * bi + fi[..., None] * br
    cr, ci = c_re.astype(F32), c_im.astype(F32)

    pr, pim_ = [jnp.ones_like(abr)], [jnp.zeros_like(abr)]
    for _ in range(t):
        r0, i0 = pr[-1], pim_[-1]
        pr.append(r0 * abr - i0 * abi)
        pim_.append(r0 * abi + i0 * abr)
    pwr, pwi = jnp.stack(pr), jnp.stack(pim_)

    cpr = cr[None] * pwr[:t, :, None, :] - ci[None] * pwi[:t, :, None, :]
    cpi = cr[None] * pwi[:t, :, None, :] + ci[None] * pwr[:t, :, None, :]
    kj = (jnp.einsum('jgap,gph->jgah', cpr, bbr, precision=hp)
          - jnp.einsum('jgap,gph->jgah', cpi, bbi, precision=hp))
    lag = jnp.arange(t)[None, :] - jnp.arange(t)[:, None]
    kk = kj[jnp.clip(lag, 0, t - 1)]
    kk = jnp.where((lag >= 0)[:, :, None, None, None], kk, 0.0)
    toep = kk.transpose(2, 0, 4, 1, 3).reshape(g, t * SSM_GROUP, t * SSM_GROUP)

    rr, ri = pwr[t - 1::-1][:t], pwi[t - 1::-1][:t]
    qr = rr[..., None] * bbr[None] - ri[..., None] * bbi[None]
    qi = rr[..., None] * bbi[None] + ri[..., None] * bbr[None]
    qr = qr.transpose(1, 0, 3, 2).reshape(g, t * SSM_GROUP, p)
    qi = qi.transpose(1, 0, 3, 2).reshape(g, t * SSM_GROUP, p)

    wr = cr[None] * pwr[1:, :, None, :] - ci[None] * pwi[1:, :, None, :]
    wi = cr[None] * pwi[1:, :, None, :] + ci[None] * pwr[1:, :, None, :]
    p_re = wr.transpose(1, 3, 0, 2).reshape(g, p, t * SSM_GROUP)
    p_im = (-wi).transpose(1, 3, 0, 2).reshape(g, p, t * SSM_GROUP)

    def pair_rows(m):
        m = m.reshape(SSM_PAIRS, 2, t * SSM_GROUP, p)
        z = jnp.zeros_like(m[:, 0])
        top = jnp.concatenate([m[:, 0], z], axis=-1)
        bot = jnp.concatenate([z, m[:, 1]], axis=-1)
        return jnp.concatenate([top, bot], axis=1)

    def pair_cols(m):
        m = m.reshape(SSM_PAIRS, 2, p, t * SSM_GROUP)
        z = jnp.zeros_like(m[:, 0])
        top = jnp.concatenate([m[:, 0], z], axis=-1)
        bot = jnp.concatenate([z, m[:, 1]], axis=-1)
        return jnp.concatenate([top, bot], axis=1)

    are = pwr[t].reshape(SSM_PAIRS, 2 * p)
    aim = pwi[t].reshape(SSM_PAIRS, 2 * p)
    return (toep.astype(BF16), pair_rows(qr).astype(BF16), pair_rows(qi).astype(BF16),
            pair_cols(p_re).astype(BF16), pair_cols(p_im).astype(BF16), are, aim)


def _merge_kernel(ya_ref, za_ref, ys_ref, u_ref, zs_ref, ga_ref, gs_ref,
                  d_ref, wglu_ref, wao_ref, wso_ref, o_ref):
    ya = ya_ref[...] * jax.nn.silu(za_ref[...])
    y = ys_ref[...] + d_ref[...] * u_ref[...]
    y = jax.nn.gelu(y)
    y = y * jax.nn.sigmoid(jnp.dot(y.astype(BF16), wglu_ref[...], preferred_element_type=F32))
    ys = y * jax.nn.silu(zs_ref[...])
    oa = jnp.dot(ya.astype(BF16), wao_ref[...], preferred_element_type=F32)
    os_ = jnp.dot(ys.astype(BF16), wso_ref[...], preferred_element_type=F32)
    merged = jax.nn.sigmoid(ga_ref[...]) * oa + jax.nn.sigmoid(gs_ref[...]) * os_
    o_ref[...] = merged.astype(o_ref.dtype)


def _merge(ya, ys, proj, d_row, wglu, wao, wso, tm):
    n = ya.shape[0]
    d = wao.shape[1]
    w1 = ATT_WIDTH
    cb = lambda off, wd: pl.BlockSpec((tm, wd), lambda i: (i, off // wd))
    const = lambda shp: pl.BlockSpec(shp, lambda i: (0, 0))
    return pl.pallas_call(
        _merge_kernel,
        grid=(n // tm,),
        in_specs=[cb(0, w1), cb(COL_ZATT, w1), cb(0, w1), cb(COL_U, w1), cb(COL_ZSSM, w1),
                  cb(COL_GATT, d), cb(COL_GSSM, d),
                  const((1, w1)), const(wglu.shape), const(wao.shape), const(wso.shape)],
        out_specs=pl.BlockSpec((tm, d), lambda i: (i, 0)),
        out_shape=jax.ShapeDtypeStruct((n, d), BF16),
        compiler_params=_cparams(("parallel",)),
        name="branch_merge",
    )(ya, proj, ys, proj, proj, proj, proj, d_row, wglu, wao, wso)


def _out_kernel(mg_ref, gp_ref, p_ref, x_ref, wout_ref, wple_ref, g_ref, b_ref, o_ref):
    mo = jnp.dot(mg_ref[...], wout_ref[...], preferred_element_type=F32)
    ple = jax.nn.sigmoid(gp_ref[...]) * jnp.dot(
        p_ref[...].astype(BF16), wple_ref[...], preferred_element_type=F32)
    r = ALPHA * x_ref[...] + mo + ple
    mu = jnp.mean(r, axis=-1, keepdims=True)
    rc = r - mu
    var = jnp.mean(rc * rc, axis=-1, keepdims=True)
    o_ref[...] = rc * lax.rsqrt(var + LN_EPS) * g_ref[...] + b_ref[...]


def _out_ln(merged, proj, p2, x2, wout, wple, g_row, b_row, tm):
    n, d = x2.shape
    const = lambda shp: pl.BlockSpec(shp, lambda i: (0, 0))
    return pl.pallas_call(
        _out_kernel,
        grid=(n // tm,),
        in_specs=[pl.BlockSpec((tm, d), lambda i: (i, 0)),
                  pl.BlockSpec((tm, d), lambda i: (i, COL_GPLE // d)),
                  pl.BlockSpec((tm, p2.shape[1]), lambda i: (i, 0)),
                  pl.BlockSpec((tm, d), lambda i: (i, 0)),
                  const(wout.shape), const(wple.shape), const((1, d)), const((1, d))],
        out_specs=pl.BlockSpec((tm, d), lambda i: (i, 0)),
        out_shape=jax.ShapeDtypeStruct((n, d), F32),
        compiler_params=_cparams(("parallel",)),
        name="out_ln",
    )(merged, proj, p2, x2, wout, wple, g_row, b_row)


def _attention_operands(proj, b, s, slopes):
    t = ATT_TILE
    nt = s // t
    scale = 1.0 / math.sqrt(DK)
    q = (proj[:, COL_Q:COL_Q + 1024] * scale).reshape(b, s, HEADS, 2, DK)
    qT = q.transpose(0, 2, 3, 4, 1)
    pad = jnp.concatenate([jnp.ones((b, HEADS, 2, 1, s), F32),
                           jnp.zeros((b, HEADS, 2, 128 - DK - 1, s), F32)], axis=3)
    qT = jnp.concatenate([qT, pad], axis=3).astype(BF16)
    k = proj[:, COL_K:COL_K + 1024].reshape(b, s, HEADS, 2, DK).transpose(0, 2, 3, 1, 4)
    rel = (jnp.arange(s, dtype=jnp.int32) % t).astype(F32)
    bias_col = slopes[None, :, None, None, None] * rel[None, None, None, :, None]
    bias_col = jnp.broadcast_to(bias_col, (b, HEADS, 2, s, 1))
    ka = jnp.concatenate([k, bias_col, jnp.zeros((b, HEADS, 2, s, 128 - DK - 1), F32)],
                         axis=-1).astype(BF16)
    ka = ka.reshape(b, HEADS, 2, nt, t, 128)
    v = proj[:, COL_V:COL_V + 1024].reshape(b, nt, t, HEADS, DV)
    vT = v.transpose(0, 3, 1, 4, 2).astype(BF16)
    return qT, ka, vT


def _layer(i, x2, p2, b, s, w_in, w_att_out, w_ssm_out, w_out, w_ple, lq1, lk1, lq2, lk2,
           subln_w, a_re, a_im, log_dt, b_re, b_im, c_re, c_im, ssm_d, w_glu, ln_g, ln_b):
    n = b * s
    lambda_init = 0.8 - 0.6 * math.exp(-0.3 * i)
    lam = (jnp.exp(jnp.sum(lq1.astype(F32) * lk1.astype(F32)))
           - jnp.exp(jnp.sum(lq2.astype(F32) * lk2.astype(F32))) + lambda_init)
    slopes = 2.0 ** (-8.0 * (jnp.arange(HEADS, dtype=F32) + 1.0) / HEADS)

    proj = _proj(x2, w_in.astype(BF16), 512, 1024)

    qT, ka, vT = _attention_operands(proj, b, s, slopes)
    y_att = _attention(slopes, lam.reshape(1), qT, ka, vT,
                       subln_w.astype(F32).reshape(DV, 1), lambda_init)
    y_att = y_att.reshape(n, ATT_WIDTH)

    nc = s // SSM_T
    u = proj[:, COL_U:COL_U + SSM_WIDTH]
    up = u.reshape(b, nc, SSM_T, SSM_PAIRS, 2, SSM_GROUP).transpose(3, 0, 1, 4, 2, 5)
    up = up.reshape(SSM_PAIRS, b, nc, 2 * SSM_T * SSM_GROUP).astype(BF16)
    mats = _ssm_matrices(a_re, a_im, log_dt, b_re, b_im, c_re, c_im)
    ncb = min(256, nc)
    yp = _ssm_scan(up, *mats, ncb)
    ys = yp.reshape(SSM_PAIRS, b, nc, 2, SSM_T, SSM_GROUP).transpose(1, 2, 4, 0, 3, 5)
    ys = ys.reshape(n, SSM_WIDTH)

    merged = _merge(y_att, ys, proj, ssm_d.astype(F32).reshape(1, SSM_WIDTH),
                    w_glu.astype(BF16), w_att_out.astype(BF16), w_ssm_out.astype(BF16), 256)
    return _out_ln(merged, proj, p2, x2, w_out.astype(BF16), w_ple.astype(BF16),
                   ln_g.astype(F32).reshape(1, -1), ln_b.astype(F32).reshape(1, -1), 256)


def kernel(x, p, w_in, w_att_out, w_ssm_out, w_out, w_ple, lambda_q1, lambda_k1, lambda_q2,
           lambda_k2, subln_w, ssm_a_re, ssm_a_im, ssm_log_dt, ssm_b_re, ssm_b_im, ssm_c_re,
           ssm_c_im, ssm_d, ssm_w_glu, ln_g, ln_b):
    b, s, d = x.shape
    x2 = x.reshape(b * s, d)
    for i in range(DEPTH):
        x2 = _layer(i, x2, p[i].reshape(b * s, PLE_DIM), b, s, w_in[i], w_att_out[i],
                    w_ssm_out[i], w_out[i], w_ple[i], lambda_q1[i], lambda_k1[i],
                    lambda_q2[i], lambda_k2[i], subln_w[i], ssm_a_re[i], ssm_a_im[i],
                    ssm_log_dt[i], ssm_b_re[i], ssm_b_im[i], ssm_c_re[i], ssm_c_im[i],
                    ssm_d[i], ssm_w_glu[i], ln_g[i], ln_b[i])
    return x2.reshape(b, s, d)
```

```python
import functools
import math

import jax
import jax.numpy as jnp
from jax import lax
from jax.experimental import pallas as pl
from jax.experimental.pallas import tpu as pltpu

F32 = jnp.float32
BF16 = jnp.bfloat16

DEPTH = 2
PLE_DIM = 256
CHUNK = 64
HEADS = 8
DK = 64
DV = 128
ATT_WIDTH = HEADS * DV
SSM_WIDTH = 1024
SSM_GROUP = 16
SSM_GROUPS = SSM_WIDTH // SSM_GROUP
SSM_STATE = 64
ALPHA = (2.0 * DEPTH) ** 0.25
LN_EPS = 1e-5
RMS_EPS = 1e-5
NEG_INF = -1e30

COL_Q, COL_K, COL_V, COL_REST = 0, 1024, 2048, 3072
COL_GATES = 6144
R_GATT, R_GSSM, R_GPLE, R_ZATT, R_U, R_ZSSM = 0, 2048, 4096, 6144, 7168, 8192
REST_COLS = 9216

ATT_TILE = 512
ATT_K = 256
BF16_EXACT_INT = 256
QKV_ROWS = 256
SSM_T = 16
SSM_PAIRS = SSM_GROUPS // 2
SSM_PAIR_W = 2 * SSM_GROUP
SSM_ROW_W = SSM_T * SSM_PAIR_W
SSM_PAIR_BLOCK = 8
SSM_CHUNKS = 128
VMEM_LIMIT = 48 * 1024 * 1024


def _cparams(sem):
    return pltpu.CompilerParams(dimension_semantics=sem, vmem_limit_bytes=VMEM_LIMIT)


def _resident(shape):
    zeros = (0,) * len(shape)
    return pl.BlockSpec(shape, lambda *_: zeros, pipeline_mode=pl.Buffered(1))


def _qkv_kernel(x_ref, wqv_ref, wk_ref, kb_ref, qT_ref, ka_ref, vT_ref):
    xb = x_ref[...].astype(BF16)
    tm = xb.shape[0]
    nt = (((1,), (1,)), ((), ()))
    qv = lax.dot_general(wqv_ref[...], xb, nt, preferred_element_type=F32)
    k = jnp.dot(xb, wk_ref[...], preferred_element_type=F32)

    row = lax.broadcasted_iota(jnp.int32, (ATT_K - 2 * DK, tm), 0)
    ones_rows = jnp.where(row < 2, 1.0, 0.0).astype(BF16)
    zeros = jnp.zeros((DK, tm), BF16)
    for h in range(HEADS):
        q1 = qv[(2 * h) * DK:(2 * h + 1) * DK].astype(BF16)
        q2 = qv[(2 * h + 1) * DK:(2 * h + 2) * DK].astype(BF16)
        qT_ref[0, h, 0] = jnp.concatenate([q1, zeros, ones_rows], axis=0)
        qT_ref[0, h, 1] = jnp.concatenate([zeros, q2, ones_rows], axis=0)
        ka_ref[0, h, :, 0:2 * DK] = k[:, h * 2 * DK:(h + 1) * 2 * DK].astype(BF16)
        ka_ref[0, h, :, 2 * DK:ATT_K] = kb_ref[h]
        v0 = HEADS * 2 * DK + h * DV
        vT_ref[0, h, 0] = qv[v0:v0 + DV].astype(BF16)


def _qkv_proj(x2, wqvT, wk, kbias, b, s):
    n, d = x2.shape
    tm = QKV_ROWS
    t = ATT_TILE
    per_b = s // tm
    sub = t // tm
    return pl.pallas_call(
        _qkv_kernel,
        grid=(n // tm,),
        in_specs=[pl.BlockSpec((tm, d), lambda i: (i, 0)),
                  _resident(wqvT.shape), _resident(wk.shape),
                  pl.BlockSpec((HEADS, tm, ATT_K - 2 * DK), lambda i: (0, i % sub, 0))],
        out_specs=[
            pl.BlockSpec((1, HEADS, 2, ATT_K, tm), lambda i: (i // per_b, 0, 0, 0, i % per_b)),
            pl.BlockSpec((1, HEADS, tm, ATT_K), lambda i: (i // per_b, 0, i % per_b, 0)),
            pl.BlockSpec((1, HEADS, 1, DV, tm),
                         lambda i: (i // per_b, 0, (i % per_b) // sub, 0, i % sub)),
        ],
        out_shape=[jax.ShapeDtypeStruct((b, HEADS, 2, ATT_K, s), BF16),
                   jax.ShapeDtypeStruct((b, HEADS, s, ATT_K), BF16),
                   jax.ShapeDtypeStruct((b, HEADS, s // t, DV, t), BF16)],
        compiler_params=_cparams(("parallel",)),
        name="qkv_proj",
    )(x2, wqvT, wk, kbias)


def _proj_kernel(x_ref, w_ref, o_ref):
    o_ref[...] = jnp.dot(x_ref[...].astype(BF16), w_ref[...],
                         preferred_element_type=F32).astype(o_ref.dtype)


def _proj(x2, w, tm, tn):
    n, d = x2.shape
    c = w.shape[1]
    return pl.pallas_call(
        _proj_kernel,
        grid=(n // tm, c // tn),
        in_specs=[pl.BlockSpec((tm, d), lambda i, j: (i, 0)),
                  pl.BlockSpec((d, tn), lambda i, j: (0, j))],
        out_specs=pl.BlockSpec((tm, tn), lambda i, j: (i, j)),
        out_shape=jax.ShapeDtypeStruct((n, c), F32),
        compiler_params=_cparams(("parallel", "arbitrary")),
        name="in_proj",
    )(x2, w)


def _attn_kernel(slope_ref, lam_ref, qT_ref, ka_ref, vT_ref, w_ref, o_ref,
                 s_scr, cm_scr, m_scr, l_scr, acc_scr, *, tile, out_scale):
    h = pl.program_id(1)
    qi = pl.program_id(2)
    slope = slope_ref[h]
    lam = lam_ref[0]

    m_scr[...] = jnp.full(m_scr.shape, NEG_INF, F32)
    l_scr[...] = jnp.zeros(l_scr.shape, F32)
    acc_scr[...] = jnp.zeros(acc_scr.shape, F32)

    def scores(j):
        kt = ka_ref[0, 0, pl.ds(pl.multiple_of(j * tile, tile), tile), :]
        for mp in range(2):
            u = jnp.dot(kt, qT_ref[0, 0, mp], preferred_element_type=F32)
            s_scr[mp] = u
            cm_scr[mp] = jnp.max(u, axis=0, keepdims=True)

    def update(mp, u, cm, shift, j):
        m_old = m_scr[mp]
        m_new = jnp.maximum(m_old, cm + shift)
        a = jnp.exp(m_old - m_new)
        p = jnp.exp(u - (m_new - shift))
        l_scr[mp] = a * l_scr[mp] + jnp.sum(p, axis=0, keepdims=True)
        pv = jnp.dot(vT_ref[0, 0, j], p.astype(BF16), preferred_element_type=F32)
        acc_scr[mp] = a * acc_scr[mp] + pv
        m_scr[mp] = m_new

    scores(0)

    def past_tile(j, carry):
        shift = slope * ((j - qi) * tile).astype(F32)
        us = [s_scr[mp] for mp in range(2)]
        cms = [cm_scr[mp] for mp in range(2)]
        scores(j + 1)
        for mp in range(2):
            update(mp, us[mp], cms[mp], shift, j)
        return carry

    lax.fori_loop(0, qi, past_tile, 0)

    ik = lax.broadcasted_iota(jnp.int32, (tile, tile), 0)
    iq = lax.broadcasted_iota(jnp.int32, (tile, tile), 1)
    allowed = (ik // CHUNK) <= (iq // CHUNK)
    fix = (-2.0 * slope) * jnp.maximum(ik - iq, 0).astype(F32)
    for mp in range(2):
        u = jnp.where(allowed, s_scr[mp] + fix, NEG_INF)
        update(mp, u, jnp.max(u, axis=0, keepdims=True), jnp.float32(0.0), qi)

    o = acc_scr[0] / l_scr[0] - lam * (acc_scr[1] / l_scr[1])
    ms = jnp.mean(o * o, axis=0, keepdims=True)
    o = o * lax.rsqrt(ms + RMS_EPS)
    o = o * w_ref[...] * out_scale
    o_ref[0] = o.T


def _attention(slopes, lam, qT, ka, vT, subln_col, lambda_init):
    b, h, _, _, s = qT.shape
    t = ATT_TILE
    nt = s // t
    kern = functools.partial(_attn_kernel, tile=t, out_scale=1.0 - lambda_init)
    return pl.pallas_call(
        kern,
        grid=(b, h, nt),
        in_specs=[
            pl.BlockSpec(memory_space=pltpu.SMEM),
            pl.BlockSpec(memory_space=pltpu.SMEM),
            pl.BlockSpec((1, 1, 2, ATT_K, t), lambda bi, hi, qi: (bi, hi, 0, 0, qi)),
            pl.BlockSpec((1, 1, s, ATT_K), lambda bi, hi, qi: (bi, hi, 0, 0)),
            pl.BlockSpec((1, 1, nt, DV, t), lambda bi, hi, qi: (bi, hi, 0, 0, 0)),
            pl.BlockSpec((DV, 1), lambda bi, hi, qi: (0, 0)),
        ],
        out_specs=pl.BlockSpec((1, t, DV), lambda bi, hi, qi: (bi, qi, hi)),
        out_shape=jax.ShapeDtypeStruct((b, s, ATT_WIDTH), F32),
        scratch_shapes=[pltpu.VMEM((2, t, t), F32),
                        pltpu.VMEM((2, 1, t), F32),
                        pltpu.VMEM((2, 1, t), F32),
                        pltpu.VMEM((2, 1, t), F32),
                        pltpu.VMEM((2, DV, t), F32)],
        compiler_params=_cparams(("parallel", "parallel", "arbitrary")),
        name="diff_attn",
    )(slopes, lam, qT, ka, vT, subln_col)


def _ssm_kernel(u_ref, toep_ref, qre_ref, qim_ref, pre_ref, pim_ref, are_ref, aim_ref,
                y_ref, ua_scr, ub_scr, up_scr, yp_scr, zre_scr, zim_scr, xre_scr, xim_scr,
                st_re, st_im, *, ncb):
    npb = SSM_PAIR_BLOCK
    pw = SSM_PAIR_W
    rows8 = 8 * SSM_T
    halves = (ua_scr, ub_scr)
    per_half = 128 // pw

    @pl.when(pl.program_id(2) == 0)
    def _():
        st_re[...] = jnp.zeros(st_re.shape, F32)
        st_im[...] = jnp.zeros(st_im.shape, F32)

    ua_scr[...] = u_ref[:, 0:128]
    ub_scr[...] = u_ref[:, 128:256]

    def gather(rb, carry):
        tok = pl.multiple_of(rb * rows8, rows8)
        row = pl.multiple_of(rb * 8, 8)
        for tau in range(SSM_T):
            for hf in range(2):
                ut = halves[hf][pl.ds(tok + tau, 8, stride=SSM_T), :]
                for q in range(per_half):
                    up_scr[hf * per_half + q, pl.ds(row, 8), tau * pw:(tau + 1) * pw] = (
                        ut[:, q * pw:(q + 1) * pw])
        return carry

    lax.fori_loop(0, ncb // 8, gather, 0)

    for pi in range(npb):
        up = up_scr[pi].astype(BF16)
        zre_scr[pl.ds(pi, ncb, stride=npb), :] = jnp.dot(
            up, qre_ref[pi], preferred_element_type=F32)
        zim_scr[pl.ds(pi, ncb, stride=npb), :] = jnp.dot(
            up, qim_ref[pi], preferred_element_type=F32)

    ar = are_ref[...]
    ai = aim_ref[...]

    def step(c, carry):
        xr, xi = carry
        row = pl.multiple_of(c * npb, npb)
        xre_scr[pl.ds(row, npb), :] = xr
        xim_scr[pl.ds(row, npb), :] = xi
        nxr = ar * xr - ai * xi + zre_scr[pl.ds(row, npb), :]
        nxi = ar * xi + ai * xr + zim_scr[pl.ds(row, npb), :]
        return nxr, nxi

    xr, xi = lax.fori_loop(0, ncb, step, (st_re[...], st_im[...]))
    st_re[...] = xr
    st_im[...] = xi

    for pi in range(npb):
        up = up_scr[pi].astype(BF16)
        xrp = xre_scr[pl.ds(pi, ncb, stride=npb), :].astype(BF16)
        xip = xim_scr[pl.ds(pi, ncb, stride=npb), :].astype(BF16)
        yp_scr[pi] = (jnp.dot(up, toep_ref[pi], preferred_element_type=F32)
                      + jnp.dot(xrp, pre_ref[pi], preferred_element_type=F32)
                      + jnp.dot(xip, pim_ref[pi], preferred_element_type=F32))

    def scatter(rb, carry):
        tok = pl.multiple_of(rb * rows8, rows8)
        row = pl.multiple_of(rb * 8, 8)
        for tau in range(SSM_T):
            for hf in range(2):
                yt = jnp.concatenate(
                    [yp_scr[hf * per_half + q, pl.ds(row, 8), tau * pw:(tau + 1) * pw]
                     for q in range(per_half)], axis=1)
                halves[hf][pl.ds(tok + tau, 8, stride=SSM_T), :] = yt
        return carry

    lax.fori_loop(0, ncb // 8, scatter, 0)
    y_ref[:, 0:128] = ua_scr[...]
    y_ref[:, 128:256] = ub_scr[...]


def _ssm_scan(rest, b, s, toep, qre, qim, pre, pim, are, aim):
    n = rest.shape[0]
    npb = SSM_PAIR_BLOCK
    ncb = min(SSM_CHUNKS, s // SSM_T)
    tok = ncb * SSM_T
    cw = npb * SSM_PAIR_W
    nblk = s // tok
    u_col0 = R_U // cw
    w = SSM_ROW_W
    kern = functools.partial(_ssm_kernel, ncb=ncb)
    return pl.pallas_call(
        kern,
        grid=(b, SSM_PAIRS // npb, nblk),
        in_specs=[
            pl.BlockSpec((tok, cw), lambda bi, gi, ti: (bi * nblk + ti, u_col0 + gi)),
            pl.BlockSpec((npb, w, w), lambda bi, gi, ti: (gi, 0, 0)),
            pl.BlockSpec((npb, w, 128), lambda bi, gi, ti: (gi, 0, 0)),
            pl.BlockSpec((npb, w, 128), lambda bi, gi, ti: (gi, 0, 0)),
            pl.BlockSpec((npb, 128, w), lambda bi, gi, ti: (gi, 0, 0)),
            pl.BlockSpec((npb, 128, w), lambda bi, gi, ti: (gi, 0, 0)),
            pl.BlockSpec((npb, 128), lambda bi, gi, ti: (gi, 0)),
            pl.BlockSpec((npb, 128), lambda bi, gi, ti: (gi, 0)),
        ],
        out_specs=pl.BlockSpec((tok, cw), lambda bi, gi, ti: (bi * nblk + ti, gi)),
        out_shape=jax.ShapeDtypeStruct((n, SSM_WIDTH), F32),
        scratch_shapes=[pltpu.VMEM((tok, 128), F32), pltpu.VMEM((tok, 128), F32),
                        pltpu.VMEM((npb, ncb, w), F32), pltpu.VMEM((npb, ncb, w), F32)]
        + [pltpu.VMEM((ncb * npb, 128), F32) for _ in range(4)]
        + [pltpu.VMEM((npb, 128), F32), pltpu.VMEM((npb, 128), F32)],
        compiler_params=_cparams(("parallel", "parallel", "arbitrary")),
        name="s5_scan",
    )(rest, toep, qre, qim, pre, pim, are, aim)


def _ssm_matrices(a_re, a_im, log_dt, b_re, b_im, c_re, c_im):
    hp = lax.Precision.HIGHEST
    g, p, t, hh = SSM_GROUPS, SSM_STATE, SSM_T, SSM_GROUP
    ar, ai = a_re.astype(F32), a_im.astype(F32)
    dt = jnp.exp(log_dt.astype(F32))[:, None]
    mag = jnp.exp(ar * dt)
    ph = ai * dt
    abr, abi = mag * jnp.cos(ph), mag * jnp.sin(ph)
    nr, ni = abr - 1.0, abi
    den = ar * ar + ai * ai
    fr = (nr * ar + ni * ai) / den
    fi = (ni * ar - nr * ai) / den
    br, bi = b_re.astype(F32), b_im.astype(F32)
    bbr = fr[..., None] * br - fi[..., None] * bi
    bbi = fr[..., None] * bi + fi[..., None] * br
    cr, ci = c_re.astype(F32), c_im.astype(F32)

    pr, pim_ = [jnp.ones_like(abr)], [jnp.zeros_like(abr)]
    for _ in range(t):
        r0, i0 = pr[-1], pim_[-1]
        pr.append(r0 * abr - i0 * abi)
        pim_.append(r0 * abi + i0 * abr)
    pwr, pwi = jnp.stack(pr), jnp.stack(pim_)

    cpr = cr[None] * pwr[:t, :, None, :] - ci[None] * pwi[:t, :, None, :]
    cpi = cr[None] * pwi[:t, :, None, :] + ci[None] * pwr[:t, :, None, :]
    kj = (jnp.einsum('jgap,gph->jgah', cpr, bbr, precision=hp)
          - jnp.einsum('jgap,gph->jgah', cpi, bbi, precision=hp))
    lag = jnp.arange(t)[None, :] - jnp.arange(t)[:, None]
    kk = kj[jnp.clip(lag, 0, t - 1)]
    kk = jnp.where((lag >= 0)[:, :, None, None, None], kk, 0.0)
    kk = kk.reshape(t, t, SSM_PAIRS, 2, hh, hh)
    eye2 = jnp.eye(2, dtype=F32)
    toep = jnp.einsum('abpgyx,gk->pagxbky', kk, eye2).reshape(SSM_PAIRS, SSM_ROW_W, SSM_ROW_W)

    rr, ri = pwr[t - 1::-1][:t], pwi[t - 1::-1][:t]
    qr = rr[..., None] * bbr[None] - ri[..., None] * bbi[None]
    qi = rr[..., None] * bbi[None] + ri[..., None] * bbr[None]

    def state_in(m):
        m = m.reshape(t, SSM_PAIRS, 2, p, hh)
        return jnp.einsum('apgsx,gk->pagxks', m, eye2).reshape(SSM_PAIRS, SSM_ROW_W, 2 * p)

    wr = cr[None] * pwr[1:, :, None, :] - ci[None] * pwi[1:, :, None, :]
    wi = cr[None] * pwi[1:, :, None, :] + ci[None] * pwr[1:, :, None, :]

    def state_out(m):
        m = m.reshape(t, SSM_PAIRS, 2, hh, p)
        return jnp.einsum('apgys,gk->pgsaky', m, eye2).reshape(SSM_PAIRS, 2 * p, SSM_ROW_W)

    are = pwr[t].reshape(SSM_PAIRS, 2 * p)
    aim = pwi[t].reshape(SSM_PAIRS, 2 * p)
    return (toep.astype(BF16), state_in(qr).astype(BF16), state_in(qi).astype(BF16),
            state_out(wr).astype(BF16), state_out(-wi).astype(BF16), are, aim)


def _merge_kernel(ya_ref, za_ref, ys_ref, u_ref, zs_ref, ga_ref, gs_ref,
                  d_ref, wglu_ref, wao_ref, wso_ref, o_ref):
    ya = ya_ref[...] * jax.nn.silu(za_ref[...])
    y = ys_ref[...] + d_ref[...] * u_ref[...]
    y = jax.nn.gelu(y)
    y = y * jax.nn.sigmoid(jnp.dot(y.astype(BF16), wglu_ref[...], preferred_element_type=F32))
    ys = y * jax.nn.silu(zs_ref[...])
    oa = jnp.dot(ya.astype(BF16), wao_ref[...], preferred_element_type=F32)
    os_ = jnp.dot(ys.astype(BF16), wso_ref[...], preferred_element_type=F32)
    merged = jax.nn.sigmoid(ga_ref[...]) * oa + jax.nn.sigmoid(gs_ref[...]) * os_
    o_ref[...] = merged.astype(o_ref.dtype)


def _merge(ya, ys, rest, d_row, wglu, wao, wso, tm):
    n = ya.shape[0]
    d = wao.shape[1]
    w1 = ATT_WIDTH
    cb = lambda off, wd: pl.BlockSpec((tm, wd), lambda i: (i, off // wd))
    return pl.pallas_call(
        _merge_kernel,
        grid=(n // tm,),
        in_specs=[cb(0, w1), cb(R_ZATT, w1), cb(0, w1), cb(R_U, w1), cb(R_ZSSM, w1),
                  cb(R_GATT, d), cb(R_GSSM, d),
                  _resident((1, w1)), _resident(wglu.shape), _resident(wao.shape),
                  _resident(wso.shape)],
        out_specs=pl.BlockSpec((tm, d), lambda i: (i, 0)),
        out_shape=jax.ShapeDtypeStruct((n, d), BF16),
        compiler_params=_cparams(("parallel",)),
        name="branch_merge",
    )(ya, rest, ys, rest, rest, rest, rest, d_row, wglu, wao, wso)


def _out_kernel(mg_ref, gp_ref, p_ref, x_ref, wout_ref, wple_ref, g_ref, b_ref, o_ref):
    mo = jnp.dot(mg_ref[...], wout_ref[...], preferred_element_type=F32)
    ple = jax.nn.sigmoid(gp_ref[...]) * jnp.dot(
        p_ref[...].astype(BF16), wple_ref[...], preferred_element_type=F32)
    r = ALPHA * x_ref[...] + mo + ple
    mu = jnp.mean(r, axis=-1, keepdims=True)
    rc = r - mu
    var = jnp.mean(rc * rc, axis=-1, keepdims=True)
    o_ref[...] = rc * lax.rsqrt(var + LN_EPS) * g_ref[...] + b_ref[...]


def _out_ln(merged, rest, p2, x2, wout, wple, g_row, b_row, tm):
    n, d = x2.shape
    return pl.pallas_call(
        _out_kernel,
        grid=(n // tm,),
        in_specs=[pl.BlockSpec((tm, d), lambda i: (i, 0)),
                  pl.BlockSpec((tm, d), lambda i: (i, R_GPLE // d)),
                  pl.BlockSpec((tm, p2.shape[1]), lambda i: (i, 0)),
                  pl.BlockSpec((tm, d), lambda i: (i, 0)),
                  _resident(wout.shape), _resident(wple.shape),
                  _resident((1, d)), _resident((1, d))],
        out_specs=pl.BlockSpec((tm, d), lambda i: (i, 0)),
        out_shape=jax.ShapeDtypeStruct((n, d), F32),
        compiler_params=_cparams(("parallel",)),
        name="out_ln",
    )(merged, rest, p2, x2, wout, wple, g_row, b_row)


def _key_bias_columns(slopes):
    t = ATT_TILE
    rel = jnp.arange(t, dtype=jnp.int32)
    parts = jnp.stack([rel % BF16_EXACT_INT, rel - rel % BF16_EXACT_INT], axis=-1).astype(F32)
    cols = slopes[:, None, None] * parts[None]
    pad = jnp.zeros((HEADS, t, ATT_K - 2 * DK - 2), F32)
    return jnp.concatenate([cols, pad], axis=-1).astype(BF16)


def _layer(i, x2, p2, b, s, w_in, w_att_out, w_ssm_out, w_out, w_ple, lq1, lk1, lq2, lk2,
           subln_w, a_re, a_im, log_dt, b_re, b_im, c_re, c_im, ssm_d, w_glu, ln_g, ln_b):
    n = b * s
    lambda_init = 0.8 - 0.6 * math.exp(-0.3 * i)
    lam = (jnp.exp(jnp.sum(lq1.astype(F32) * lk1.astype(F32)))
           - jnp.exp(jnp.sum(lq2.astype(F32) * lk2.astype(F32))) + lambda_init)
    slopes = 2.0 ** (-8.0 * (jnp.arange(HEADS, dtype=F32) + 1.0) / HEADS)

    scale = 1.0 / math.sqrt(DK)
    wqvT = jnp.concatenate([w_in[:, COL_Q:COL_K] * scale, w_in[:, COL_V:COL_REST]],
                           axis=1).T.astype(BF16)
    wk = w_in[:, COL_K:COL_V].astype(BF16)
    qT, ka, vT = _qkv_proj(x2, wqvT, wk, _key_bias_columns(slopes), b, s)
    w_rest = jnp.concatenate([w_in[:, COL_GATES:], w_in[:, COL_REST:COL_GATES]], axis=1)
    rest = _proj(x2, w_rest.astype(BF16), 512, 1024)

    y_att = _attention(slopes, lam.reshape(1), qT, ka, vT,
                       subln_w.astype(F32).reshape(DV, 1), lambda_init)
    y_att = y_att.reshape(n, ATT_WIDTH)

    mats = _ssm_matrices(a_re, a_im, log_dt, b_re, b_im, c_re, c_im)
    ys = _ssm_scan(rest, b, s, *mats)

    merged = _merge(y_att, ys, rest, ssm_d.astype(F32).reshape(1, SSM_WIDTH),
                    w_glu.astype(BF16), w_att_out.astype(BF16), w_ssm_out.astype(BF16), 256)
    return _out_ln(merged, rest, p2, x2, w_out.astype(BF16), w_ple.astype(BF16),
                   ln_g.astype(F32).reshape(1, -1), ln_b.astype(F32).reshape(1, -1), 256)


def kernel(x, p, w_in, w_att_out, w_ssm_out, w_out, w_ple, lambda_q1, lambda_k1, lambda_q2,
           lambda_k2, subln_w, ssm_a_re, ssm_a_im, ssm_log_dt, ssm_b_re, ssm_b_im, ssm_c_re,
           ssm_c_im, ssm_d, ssm_w_glu, ln_g, ln_b):
    b, s, d = x.shape
    x2 = x.reshape(b * s, d)
    for i in range(DEPTH):
        x2 = _layer(i, x2, p[i].reshape(b * s, PLE_DIM), b, s, w_in[i], w_att_out[i],
                    w_ssm_out[i], w_out[i], w_ple[i], lambda_q1[i], lambda_k1[i],
                    lambda_q2[i], lambda_k2[i], subln_w[i], ssm_a_re[i], ssm_a_im[i],
                    ssm_log_dt[i], ssm_b_re[i], ssm_b_im[i], ssm_c_re[i], ssm_c_im[i],
                    ssm_d[i], ssm_w_glu[i], ln_g[i], ln_b[i])
    return x2.reshape(b, s, d)
```

```python
import functools
import math

import jax
import jax.numpy as jnp
from jax import lax
from jax.experimental import pallas as pl
from jax.experimental.pallas import tpu as pltpu

F32 = jnp.float32
BF16 = jnp.bfloat16

DEPTH = 2
PLE_DIM = 256
CHUNK = 64
HEADS = 8
DK = 64
DV = 128
ATT_WIDTH = HEADS * DV
SSM_WIDTH = 1024
SSM_GROUP = 16
SSM_GROUPS = SSM_WIDTH // SSM_GROUP
SSM_STATE = 64
ALPHA = (2.0 * DEPTH) ** 0.25
LN_EPS = 1e-5
RMS_EPS = 1e-5
NEG_INF = -1e30

COL_Q, COL_K, COL_V, COL_REST = 0, 1024, 2048, 3072
COL_GATES = 6144
R_GATT, R_GSSM, R_GPLE, R_ZATT, R_U, R_ZSSM = 0, 2048, 4096, 6144, 7168, 8192
REST_COLS = 9216

ATT_TILE = 512
ATT_K = 256
BIAS_PARTS = 3
DV_AUG = DV + 16
LOG2E = 1.4426950408889634
QKV_ROWS = 256
SSM_T = 16
SSM_PAIRS = SSM_GROUPS // 2
SSM_PAIR_W = 2 * SSM_GROUP
SSM_ROW_W = SSM_T * SSM_PAIR_W
SSM_PAIR_BLOCK = 8
SSM_CHUNKS = 128
VMEM_LIMIT = 48 * 1024 * 1024


def _cparams(sem):
    return pltpu.CompilerParams(dimension_semantics=sem, vmem_limit_bytes=VMEM_LIMIT)


def _resident(stacked_shape, layer, col_block=0):
    idx = (layer, 0, col_block)
    return pl.BlockSpec((None,) + tuple(stacked_shape[1:]), lambda *_: idx,
                        pipeline_mode=pl.Buffered(1))


def _qkv_kernel(x_ref, wqv_ref, wk_ref, kb_ref, qT_ref, ka_ref, vT_ref):
    xb = x_ref[...].astype(BF16)
    tm = xb.shape[0]
    nt = (((1,), (1,)), ((), ()))
    qv = lax.dot_general(wqv_ref[...], xb, nt, preferred_element_type=F32)
    k = jnp.dot(xb, wk_ref[...], preferred_element_type=F32)

    row = lax.broadcasted_iota(jnp.int32, (ATT_K - 2 * DK, tm), 0)
    ones_rows = jnp.where(row < BIAS_PARTS, 1.0, 0.0).astype(BF16)
    zeros = jnp.zeros((DK, tm), BF16)
    sum_rows = jnp.ones((DV_AUG - DV, tm), BF16)
    for h in range(HEADS):
        q1 = qv[(2 * h) * DK:(2 * h + 1) * DK].astype(BF16)
        q2 = qv[(2 * h + 1) * DK:(2 * h + 2) * DK].astype(BF16)
        qT_ref[0, h, 0] = jnp.concatenate([q1, zeros, ones_rows], axis=0)
        qT_ref[0, h, 1] = jnp.concatenate([zeros, q2, ones_rows], axis=0)
        ka_ref[0, h, :, 0:2 * DK] = k[:, h * 2 * DK:(h + 1) * 2 * DK].astype(BF16)
        ka_ref[0, h, :, 2 * DK:ATT_K] = kb_ref[h]
        v0 = HEADS * 2 * DK + h * DV
        vT_ref[0, h, 0] = jnp.concatenate([qv[v0:v0 + DV].astype(BF16), sum_rows], axis=0)


def _qkv_proj(x2, wqvT, w_in, kbias, layer, b, s):
    n, d = x2.shape
    tm = QKV_ROWS
    t = ATT_TILE
    per_b = s // tm
    sub = t // tm
    k_width = COL_V - COL_K
    return pl.pallas_call(
        _qkv_kernel,
        grid=(n // tm,),
        in_specs=[pl.BlockSpec((tm, d), lambda i: (i, 0)),
                  _resident(wqvT.shape, layer),
                  _resident((DEPTH, d, k_width), layer, COL_K // k_width),
                  pl.BlockSpec((HEADS, tm, ATT_K - 2 * DK), lambda i: (0, i % sub, 0))],
        out_specs=[
            pl.BlockSpec((1, HEADS, 2, ATT_K, tm), lambda i: (i // per_b, 0, 0, 0, i % per_b)),
            pl.BlockSpec((1, HEADS, tm, ATT_K), lambda i: (i // per_b, 0, i % per_b, 0)),
            pl.BlockSpec((1, HEADS, 1, DV_AUG, tm),
                         lambda i: (i // per_b, 0, (i % per_b) // sub, 0, i % sub)),
        ],
        out_shape=[jax.ShapeDtypeStruct((b, HEADS, 2, ATT_K, s), BF16),
                   jax.ShapeDtypeStruct((b, HEADS, s, ATT_K), BF16),
                   jax.ShapeDtypeStruct((b, HEADS, s // t, DV_AUG, t), BF16)],
        compiler_params=_cparams(("parallel",)),
        name="qkv_proj",
    )(x2, wqvT, w_in, kbias)


def _proj_kernel(x_ref, w_ref, o_ref):
    o_ref[...] = jnp.dot(x_ref[...].astype(BF16), w_ref[...],
                         preferred_element_type=F32).astype(o_ref.dtype)


def _rest_proj(x2, w_in, layer, tm, tn):
    n, d = x2.shape
    gate_blocks = (w_in.shape[2] - COL_GATES) // tn
    first_gate = COL_GATES // tn
    first_rest = COL_REST // tn

    def w_block(i, j):
        return (layer, 0, jnp.where(j < gate_blocks, j + first_gate, j - gate_blocks + first_rest))

    return pl.pallas_call(
        _proj_kernel,
        grid=(n // tm, REST_COLS // tn),
        in_specs=[pl.BlockSpec((tm, d), lambda i, j: (i, 0)),
                  pl.BlockSpec((None, d, tn), w_block)],
        out_specs=pl.BlockSpec((tm, tn), lambda i, j: (i, j)),
        out_shape=jax.ShapeDtypeStruct((n, REST_COLS), F32),
        compiler_params=_cparams(("parallel", "arbitrary")),
        name="in_proj",
    )(x2, w_in)


def _attn_kernel(slope_ref, lam_ref, qT_ref, ka_ref, vT_ref, w_ref, o_ref,
                 s_scr, cm_scr, m_scr, acc_scr, *, tile, layer, out_scale):
    h = pl.program_id(1)
    qi = pl.program_id(2)
    slope = slope_ref[h]
    lam = lam_ref[layer]

    m_scr[...] = jnp.full(m_scr.shape, NEG_INF, F32)
    acc_scr[...] = jnp.zeros(acc_scr.shape, F32)

    def scores(j, slot):
        kt = ka_ref[0, 0, pl.ds(pl.multiple_of(j * tile, tile), tile), :]
        for mp in range(2):
            u = jnp.dot(kt, qT_ref[0, 0, mp], preferred_element_type=F32)
            s_scr[slot, mp] = u
            cm_scr[slot, mp] = jnp.max(u, axis=0, keepdims=True)

    def update(mp, u, cm, shift, j):
        m_old = m_scr[mp]
        m_new = jnp.maximum(m_old, cm + shift)
        a = jnp.exp2(m_old - m_new)
        p = jnp.exp2(u - (m_new - shift))
        pv = jnp.dot(vT_ref[0, 0, j], p.astype(BF16), preferred_element_type=F32)
        acc_scr[mp] = a * acc_scr[mp] + pv
        m_scr[mp] = m_new

    scores(0, 0)

    def past_tile(j, src, dst):
        shift = slope * ((j - qi) * tile).astype(F32)
        us = [s_scr[src, mp] for mp in range(2)]
        cms = [cm_scr[src, mp] for mp in range(2)]
        scores(j + 1, dst)
        for mp in range(2):
            update(mp, us[mp], cms[mp], shift, j)

    def past_pair(jj, carry):
        past_tile(2 * jj, 0, 1)
        past_tile(2 * jj + 1, 1, 0)
        return carry

    lax.fori_loop(0, qi // 2, past_pair, 0)

    @pl.when(qi % 2 == 1)
    def _():
        past_tile(qi - 1, 0, 0)

    ik = lax.broadcasted_iota(jnp.int32, (tile, tile), 0)
    iq = lax.broadcasted_iota(jnp.int32, (tile, tile), 1)
    allowed = (ik // CHUNK) <= (iq // CHUNK)
    fix = (-2.0 * slope) * jnp.maximum(ik - iq, 0).astype(F32)
    for mp in range(2):
        u = jnp.where(allowed, s_scr[0, mp] + fix, NEG_INF)
        update(mp, u, jnp.max(u, axis=0, keepdims=True), jnp.float32(0.0), qi)

    o = (acc_scr[0, 0:DV] / acc_scr[0, DV:DV + 1]
         - lam * (acc_scr[1, 0:DV] / acc_scr[1, DV:DV + 1]))
    ms = jnp.mean(o * o, axis=0, keepdims=True)
    o = o * lax.rsqrt(ms + RMS_EPS)
    o = o * w_ref[...] * out_scale
    o_ref[0] = o.T


def _attention(slopes, lam, qT, ka, vT, subln_col, layer, lambda_init):
    b, h, _, _, s = qT.shape
    t = ATT_TILE
    nt = s // t
    kern = functools.partial(_attn_kernel, tile=t, layer=layer, out_scale=1.0 - lambda_init)
    return pl.pallas_call(
        kern,
        grid=(b, h, nt),
        in_specs=[
            pl.BlockSpec(memory_space=pltpu.SMEM),
            pl.BlockSpec(memory_space=pltpu.SMEM),
            pl.BlockSpec((1, 1, 2, ATT_K, t), lambda bi, hi, qi: (bi, hi, 0, 0, qi)),
            pl.BlockSpec((1, 1, s, ATT_K), lambda bi, hi, qi: (bi, hi, 0, 0)),
            pl.BlockSpec((1, 1, nt, DV_AUG, t), lambda bi, hi, qi: (bi, hi, 0, 0, 0)),
            pl.BlockSpec((None, DV, 1), lambda bi, hi, qi: (layer, 0, 0)),
        ],
        out_specs=pl.BlockSpec((1, t, DV), lambda bi, hi, qi: (bi, qi, hi)),
        out_shape=jax.ShapeDtypeStruct((b, s, ATT_WIDTH), F32),
        scratch_shapes=[pltpu.VMEM((2, 2, t, t), F32),
                        pltpu.VMEM((2, 2, 1, t), F32),
                        pltpu.VMEM((2, 1, t), F32),
                        pltpu.VMEM((2, DV_AUG, t), F32)],
        compiler_params=_cparams(("parallel", "parallel", "arbitrary")),
        name="diff_attn",
    )(slopes, lam, qT, ka, vT, subln_col)


def _ssm_kernel(u_ref, toep_ref, qre_ref, qim_ref, pre_ref, pim_ref, are_ref, aim_ref,
                y_ref, ua_scr, ub_scr, up_scr, yp_scr, zre_scr, zim_scr, xre_scr, xim_scr,
                st_re, st_im, *, ncb):
    npb = SSM_PAIR_BLOCK
    pw = SSM_PAIR_W
    rows8 = 8 * SSM_T
    halves = (ua_scr, ub_scr)
    per_half = 128 // pw

    @pl.when(pl.program_id(2) == 0)
    def _():
        st_re[...] = jnp.zeros(st_re.shape, F32)
        st_im[...] = jnp.zeros(st_im.shape, F32)

    ua_scr[...] = u_ref[:, 0:128]
    ub_scr[...] = u_ref[:, 128:256]

    def gather(rb, carry):
        tok = pl.multiple_of(rb * rows8, rows8)
        row = pl.multiple_of(rb * 8, 8)
        for tau in range(SSM_T):
            for hf in range(2):
                ut = halves[hf][pl.ds(tok + tau, 8, stride=SSM_T), :]
                for q in range(per_half):
                    up_scr[hf * per_half + q, pl.ds(row, 8), tau * pw:(tau + 1) * pw] = (
                        ut[:, q * pw:(q + 1) * pw])
        return carry

    lax.fori_loop(0, ncb // 8, gather, 0)

    for pi in range(npb):
        up = up_scr[pi].astype(BF16)
        zre_scr[pl.ds(pi, ncb, stride=npb), :] = jnp.dot(
            up, qre_ref[pi], preferred_element_type=F32)
        zim_scr[pl.ds(pi, ncb, stride=npb), :] = jnp.dot(
            up, qim_ref[pi], preferred_element_type=F32)

    ar = are_ref[...]
    ai = aim_ref[...]

    def step(c, carry):
        xr, xi = carry
        row = pl.multiple_of(c * npb, npb)
        xre_scr[pl.ds(row, npb), :] = xr
        xim_scr[pl.ds(row, npb), :] = xi
        nxr = ar * xr - ai * xi + zre_scr[pl.ds(row, npb), :]
        nxi = ar * xi + ai * xr + zim_scr[pl.ds(row, npb), :]
        return nxr, nxi

    xr, xi = lax.fori_loop(0, ncb, step, (st_re[...], st_im[...]))
    st_re[...] = xr
    st_im[...] = xi

    for pi in range(npb):
        up = up_scr[pi].astype(BF16)
        xrp = xre_scr[pl.ds(pi, ncb, stride=npb), :].astype(BF16)
        xip = xim_scr[pl.ds(pi, ncb, stride=npb), :].astype(BF16)
        yp_scr[pi] = (jnp.dot(up, toep_ref[pi], preferred_element_type=F32)
                      + jnp.dot(xrp, pre_ref[pi], preferred_element_type=F32)
                      + jnp.dot(xip, pim_ref[pi], preferred_element_type=F32))

    def scatter(rb, carry):
        tok = pl.multiple_of(rb * rows8, rows8)
        row = pl.multiple_of(rb * 8, 8)
        for tau in range(SSM_T):
            for hf in range(2):
                yt = jnp.concatenate(
                    [yp_scr[hf * per_half + q, pl.ds(row, 8), tau * pw:(tau + 1) * pw]
                     for q in range(per_half)], axis=1)
                halves[hf][pl.ds(tok + tau, 8, stride=SSM_T), :] = yt
        return carry

    lax.fori_loop(0, ncb // 8, scatter, 0)
    y_ref[:, 0:128] = ua_scr[...]
    y_ref[:, 128:256] = ub_scr[...]


def _ssm_scan(rest, layer, b, s, toep, qre, qim, pre, pim, are, aim):
    n = rest.shape[0]
    npb = SSM_PAIR_BLOCK
    ncb = min(SSM_CHUNKS, s // SSM_T)
    tok = ncb * SSM_T
    cw = npb * SSM_PAIR_W
    nblk = s // tok
    u_col0 = R_U // cw
    w = SSM_ROW_W
    kern = functools.partial(_ssm_kernel, ncb=ncb)
    return pl.pallas_call(
        kern,
        grid=(b, SSM_PAIRS // npb, nblk),
        in_specs=[
            pl.BlockSpec((tok, cw), lambda bi, gi, ti: (bi * nblk + ti, u_col0 + gi)),
            pl.BlockSpec((None, npb, w, w), lambda bi, gi, ti: (layer, gi, 0, 0)),
            pl.BlockSpec((None, npb, w, 128), lambda bi, gi, ti: (layer, gi, 0, 0)),
            pl.BlockSpec((None, npb, w, 128), lambda bi, gi, ti: (layer, gi, 0, 0)),
            pl.BlockSpec((None, npb, 128, w), lambda bi, gi, ti: (layer, gi, 0, 0)),
            pl.BlockSpec((None, npb, 128, w), lambda bi, gi, ti: (layer, gi, 0, 0)),
            pl.BlockSpec((None, npb, 128), lambda bi, gi, ti: (layer, gi, 0)),
            pl.BlockSpec((None, npb, 128), lambda bi, gi, ti: (layer, gi, 0)),
        ],
        out_specs=pl.BlockSpec((tok, cw), lambda bi, gi, ti: (bi * nblk + ti, gi)),
        out_shape=jax.ShapeDtypeStruct((n, SSM_WIDTH), F32),
        scratch_shapes=[pltpu.VMEM((tok, 128), F32), pltpu.VMEM((tok, 128), F32),
                        pltpu.VMEM((npb, ncb, w), F32), pltpu.VMEM((npb, ncb, w), F32)]
        + [pltpu.VMEM((ncb * npb, 128), F32) for _ in range(4)]
        + [pltpu.VMEM((npb, 128), F32), pltpu.VMEM((npb, 128), F32)],
        compiler_params=_cparams(("parallel", "parallel", "arbitrary")),
        name="s5_scan",
    )(rest, toep, qre, qim, pre, pim, are, aim)


def _ssm_matrices(a_re, a_im, log_dt, b_re, b_im, c_re, c_im):
    hp = lax.Precision.HIGHEST
    g, p, t, hh = SSM_GROUPS, SSM_STATE, SSM_T, SSM_GROUP
    ar, ai = a_re.astype(F32), a_im.astype(F32)
    dt = jnp.exp(log_dt.astype(F32))[:, None]
    mag = jnp.exp(ar * dt)
    ph = ai * dt
    abr, abi = mag * jnp.cos(ph), mag * jnp.sin(ph)
    nr, ni = abr - 1.0, abi
    den = ar * ar + ai * ai
    fr = (nr * ar + ni * ai) / den
    fi = (ni * ar - nr * ai) / den
    br, bi = b_re.astype(F32), b_im.astype(F32)
    bbr = fr[..., None] * br - fi[..., None] * bi
    bbi = fr[..., None] * bi + fi[..., None] * br
    cr, ci = c_re.astype(F32), c_im.astype(F32)

    jj = jnp.arange(t + 1, dtype=F32)[:, None, None]
    mag_j = jnp.exp(jj * (ar * dt)[None])
    pwr, pwi = mag_j * jnp.cos(jj * ph[None]), mag_j * jnp.sin(jj * ph[None])

    cpr = cr[None] * pwr[:t, :, None, :] - ci[None] * pwi[:t, :, None, :]
    cpi = cr[None] * pwi[:t, :, None, :] + ci[None] * pwr[:t, :, None, :]
    kj = (jnp.einsum('jgap,gph->jgah', cpr, bbr, precision=hp)
          - jnp.einsum('jgap,gph->jgah', cpi, bbi, precision=hp))
    lag = jnp.arange(t)[None, :] - jnp.arange(t)[:, None]
    kk = kj[jnp.clip(lag, 0, t - 1)]
    kk = jnp.where((lag >= 0)[:, :, None, None, None], kk, 0.0)
    kk = kk.reshape(t, t, SSM_PAIRS, 2, hh, hh)
    eye2 = jnp.eye(2, dtype=F32)
    toep = jnp.einsum('abpgyx,gk->pagxbky', kk, eye2).reshape(SSM_PAIRS, SSM_ROW_W, SSM_ROW_W)

    rr, ri = pwr[t - 1::-1][:t], pwi[t - 1::-1][:t]
    qr = rr[..., None] * bbr[None] - ri[..., None] * bbi[None]
    qi = rr[..., None] * bbi[None] + ri[..., None] * bbr[None]

    def state_in(m):
        m = m.reshape(t, SSM_PAIRS, 2, p, hh)
        return jnp.einsum('apgsx,gk->pagxks', m, eye2).reshape(SSM_PAIRS, SSM_ROW_W, 2 * p)

    wr = cr[None] * pwr[1:, :, None, :] - ci[None] * pwi[1:, :, None, :]
    wi = cr[None] * pwi[1:, :, None, :] + ci[None] * pwr[1:, :, None, :]

    def state_out(m):
        m = m.reshape(t, SSM_PAIRS, 2, hh, p)
        return jnp.einsum('apgys,gk->pgsaky', m, eye2).reshape(SSM_PAIRS, 2 * p, SSM_ROW_W)

    are = pwr[t].reshape(SSM_PAIRS, 2 * p)
    aim = pwi[t].reshape(SSM_PAIRS, 2 * p)
    return (toep.astype(BF16), state_in(qr).astype(BF16), state_in(qi).astype(BF16),
            state_out(wr).astype(BF16), state_out(-wi).astype(BF16), are, aim)


def _merge_kernel(ya_ref, za_ref, ys_ref, u_ref, zs_ref, ga_ref, gs_ref,
                  d_ref, wglu_ref, wao_ref, wso_ref, o_ref):
    ya = ya_ref[...] * jax.nn.silu(za_ref[...])
    y = ys_ref[...] + d_ref[...] * u_ref[...]
    y = jax.nn.gelu(y)
    y = y * jax.nn.sigmoid(jnp.dot(y.astype(BF16), wglu_ref[...], preferred_element_type=F32))
    ys = y * jax.nn.silu(zs_ref[...])
    oa = jnp.dot(ya.astype(BF16), wao_ref[...], preferred_element_type=F32)
    os_ = jnp.dot(ys.astype(BF16), wso_ref[...], preferred_element_type=F32)
    merged = jax.nn.sigmoid(ga_ref[...]) * oa + jax.nn.sigmoid(gs_ref[...]) * os_
    o_ref[...] = merged.astype(o_ref.dtype)


def _merge(ya, ys, rest, d_row, wglu, wao, wso, layer, tm):
    n = ya.shape[0]
    d = wao.shape[2]
    w1 = ATT_WIDTH
    cb = lambda off, wd: pl.BlockSpec((tm, wd), lambda i: (i, off // wd))
    return pl.pallas_call(
        _merge_kernel,
        grid=(n // tm,),
        in_specs=[cb(0, w1), cb(R_ZATT, w1), cb(0, w1), cb(R_U, w1), cb(R_ZSSM, w1),
                  cb(R_GATT, d), cb(R_GSSM, d),
                  _resident(d_row.shape, layer), _resident(wglu.shape, layer),
                  _resident(wao.shape, layer), _resident(wso.shape, layer)],
        out_specs=pl.BlockSpec((tm, d), lambda i: (i, 0)),
        out_shape=jax.ShapeDtypeStruct((n, d), BF16),
        compiler_params=_cparams(("parallel",)),
        name="branch_merge",
    )(ya, rest, ys, rest, rest, rest, rest, d_row, wglu, wao, wso)


def _out_kernel(mg_ref, gp_ref, p_ref, x_ref, wout_ref, wple_ref, g_ref, b_ref, o_ref):
    mo = jnp.dot(mg_ref[...], wout_ref[...], preferred_element_type=F32)
    ple = jax.nn.sigmoid(gp_ref[...]) * jnp.dot(
        p_ref[...].astype(BF16), wple_ref[...], preferred_element_type=F32)
    r = ALPHA * x_ref[...] + mo + ple
    mu = jnp.mean(r, axis=-1, keepdims=True)
    rc = r - mu
    var = jnp.mean(rc * rc, axis=-1, keepdims=True)
    o_ref[...] = rc * lax.rsqrt(var + LN_EPS) * g_ref[...] + b_ref[...]


def _out_ln(merged, rest, p3, x2, wout, wple, g_row, b_row, layer, tm):
    n, d = x2.shape
    return pl.pallas_call(
        _out_kernel,
        grid=(n // tm,),
        in_specs=[pl.BlockSpec((tm, d), lambda i: (i, 0)),
                  pl.BlockSpec((tm, d), lambda i: (i, R_GPLE // d)),
                  pl.BlockSpec((None, tm, p3.shape[2]), lambda i: (layer, i, 0)),
                  pl.BlockSpec((tm, d), lambda i: (i, 0)),
                  _resident(wout.shape, layer), _resident(wple.shape, layer),
                  _resident(g_row.shape, layer), _resident(b_row.shape, layer)],
        out_specs=pl.BlockSpec((tm, d), lambda i: (i, 0)),
        out_shape=jax.ShapeDtypeStruct((n, d), F32),
        compiler_params=_cparams(("parallel",)),
        name="out_ln",
    )(merged, rest, p3, x2, wout, wple, g_row, b_row)


def _key_bias_columns(slopes):
    t = ATT_TILE
    v = slopes[:, None] * jnp.arange(t, dtype=F32)[None, :]
    parts = []
    for _ in range(BIAS_PARTS):
        top = lax.bitcast_convert_type(
            lax.bitcast_convert_type(v, jnp.uint32) & jnp.uint32(0xFFFF0000), F32)
        parts.append(top)
        v = v - top
    cols = jnp.stack(parts, axis=-1)
    pad = jnp.zeros((HEADS, t, ATT_K - 2 * DK - BIAS_PARTS), F32)
    return jnp.concatenate([cols, pad], axis=-1).astype(BF16)


def kernel(x, p, w_in, w_att_out, w_ssm_out, w_out, w_ple, lambda_q1, lambda_k1, lambda_q2,
           lambda_k2, subln_w, ssm_a_re, ssm_a_im, ssm_log_dt, ssm_b_re, ssm_b_im, ssm_c_re,
           ssm_c_im, ssm_d, ssm_w_glu, ln_g, ln_b):
    b, s, d = x.shape
    n = b * s
    x2 = x.reshape(n, d)
    p3 = p.reshape(DEPTH, n, PLE_DIM)

    lambda_inits = [0.8 - 0.6 * math.exp(-0.3 * i) for i in range(DEPTH)]
    lam = (jnp.exp(jnp.sum(lambda_q1.astype(F32) * lambda_k1.astype(F32), axis=-1))
           - jnp.exp(jnp.sum(lambda_q2.astype(F32) * lambda_k2.astype(F32), axis=-1))
           + jnp.asarray(lambda_inits, F32))
    slopes = LOG2E * 2.0 ** (-8.0 * (jnp.arange(HEADS, dtype=F32) + 1.0) / HEADS)
    kbias = _key_bias_columns(slopes)
    scale = LOG2E / math.sqrt(DK)
    wqvT = jnp.concatenate([w_in[:, :, COL_Q:COL_K] * scale, w_in[:, :, COL_V:COL_REST]],
                           axis=2).astype(BF16).transpose(0, 2, 1)
    w_in_b = w_in.astype(BF16)
    w_glu_b, w_ao_b, w_so_b = (ssm_w_glu.astype(BF16), w_att_out.astype(BF16),
                               w_ssm_out.astype(BF16))
    w_out_b, w_ple_b = w_out.astype(BF16), w_ple.astype(BF16)
    subln_col = subln_w.astype(F32).reshape(DEPTH, DV, 1)
    d_row = ssm_d.astype(F32).reshape(DEPTH, 1, SSM_WIDTH)
    g_row = ln_g.astype(F32).reshape(DEPTH, 1, d)
    b_row = ln_b.astype(F32).reshape(DEPTH, 1, d)
    mats = jax.vmap(_ssm_matrices)(ssm_a_re, ssm_a_im, ssm_log_dt, ssm_b_re, ssm_b_im,
                                   ssm_c_re, ssm_c_im)

    for i in range(DEPTH):
        qT, ka, vT = _qkv_proj(x2, wqvT, w_in_b, kbias, i, b, s)
        rest = _rest_proj(x2, w_in_b, i, 1024, 1024)
        y_att = _attention(slopes, lam, qT, ka, vT, subln_col, i, lambda_inits[i])
        ys = _ssm_scan(rest, i, b, s, *mats)
        merged = _merge(y_att.reshape(n, ATT_WIDTH), ys, rest, d_row, w_glu_b, w_ao_b, w_so_b,
                        i, 256)
        x2 = _out_ln(merged, rest, p3, x2, w_out_b, w_ple_b, g_row, b_row, i, 256)
    return x2.reshape(b, s, d)
```

```python
import functools
import math

import jax
import jax.numpy as jnp
from jax import lax
from jax.experimental import pallas as pl
from jax.experimental.pallas import tpu as pltpu

F32 = jnp.float32
BF16 = jnp.bfloat16

DEPTH = 2
PLE_DIM = 256
CHUNK = 64
HEADS = 8
DK = 64
DV = 128
ATT_WIDTH = HEADS * DV
SSM_WIDTH = 1024
SSM_GROUP = 16
SSM_GROUPS = SSM_WIDTH // SSM_GROUP
SSM_STATE = 64
ALPHA = (2.0 * DEPTH) ** 0.25
LN_EPS = 1e-5
RMS_EPS = 1e-5
NEG_INF = -1e30

COL_Q, COL_K, COL_V, COL_REST = 0, 1024, 2048, 3072
COL_GATES = 6144
R_GATT, R_GSSM, R_GPLE, R_ZATT, R_U, R_ZSSM = 0, 2048, 4096, 6144, 7168, 8192
REST_COLS = 9216

ATT_TILE = 512
ATT_K = 256
ATT_UNROLL = 4
BIAS_PARTS = 3
DV_AUG = DV + 16
LOG2E = 1.4426950408889634
QKV_ROWS = 256
SSM_T = 16
SSM_PAIRS = SSM_GROUPS // 2
SSM_PAIR_W = 2 * SSM_GROUP
SSM_ROW_W = SSM_T * SSM_PAIR_W
SSM_PAIR_BLOCK = 8
SSM_CHUNKS = 128
VMEM_LIMIT = 48 * 1024 * 1024


def _cparams(sem):
    return pltpu.CompilerParams(dimension_semantics=sem, vmem_limit_bytes=VMEM_LIMIT)


def _resident(stacked_shape, layer, col_block=0):
    idx = (layer, 0, col_block)
    return pl.BlockSpec((None,) + tuple(stacked_shape[1:]), lambda *_: idx,
                        pipeline_mode=pl.Buffered(1))


def _qkv_kernel(x_ref, wqv_ref, wk_ref, kb_ref, qT_ref, ka_ref, vT_ref):
    xb = x_ref[...].astype(BF16)
    tm = xb.shape[0]
    nt = (((1,), (1,)), ((), ()))
    qv = lax.dot_general(wqv_ref[...], xb, nt, preferred_element_type=F32)
    k = jnp.dot(xb, wk_ref[...], preferred_element_type=F32)

    row = lax.broadcasted_iota(jnp.int32, (ATT_K - 2 * DK, tm), 0)
    ones_rows = jnp.where(row < BIAS_PARTS, 1.0, 0.0).astype(BF16)
    zeros = jnp.zeros((DK, tm), BF16)
    sum_rows = jnp.ones((DV_AUG - DV, tm), BF16)
    for h in range(HEADS):
        q1 = qv[(2 * h) * DK:(2 * h + 1) * DK].astype(BF16)
        q2 = qv[(2 * h + 1) * DK:(2 * h + 2) * DK].astype(BF16)
        qT_ref[0, h, 0] = jnp.concatenate([q1, zeros, ones_rows], axis=0)
        qT_ref[0, h, 1] = jnp.concatenate([zeros, q2, ones_rows], axis=0)
        ka_ref[0, h, :, 0:2 * DK] = k[:, h * 2 * DK:(h + 1) * 2 * DK].astype(BF16)
        ka_ref[0, h, :, 2 * DK:ATT_K] = kb_ref[h]
        v0 = HEADS * 2 * DK + h * DV
        vT_ref[0, h, 0] = jnp.concatenate([qv[v0:v0 + DV].astype(BF16), sum_rows], axis=0)


def _qkv_proj(x2, wqvT, w_in, kbias, layer, b, s):
    n, d = x2.shape
    tm = QKV_ROWS
    t = ATT_TILE
    per_b = s // tm
    sub = t // tm
    k_width = COL_V - COL_K
    return pl.pallas_call(
        _qkv_kernel,
        grid=(n // tm,),
        in_specs=[pl.BlockSpec((tm, d), lambda i: (i, 0)),
                  _resident(wqvT.shape, layer),
                  _resident((DEPTH, d, k_width), layer, COL_K // k_width),
                  pl.BlockSpec((HEADS, tm, ATT_K - 2 * DK), lambda i: (0, i % sub, 0))],
        out_specs=[
            pl.BlockSpec((1, HEADS, 2, ATT_K, tm), lambda i: (i // per_b, 0, 0, 0, i % per_b)),
            pl.BlockSpec((1, HEADS, tm, ATT_K), lambda i: (i // per_b, 0, i % per_b, 0)),
            pl.BlockSpec((1, HEADS, 1, DV_AUG, tm),
                         lambda i: (i // per_b, 0, (i % per_b) // sub, 0, i % sub)),
        ],
        out_shape=[jax.ShapeDtypeStruct((b, HEADS, 2, ATT_K, s), BF16),
                   jax.ShapeDtypeStruct((b, HEADS, s, ATT_K), BF16),
                   jax.ShapeDtypeStruct((b, HEADS, s // t, DV_AUG, t), BF16)],
        compiler_params=_cparams(("parallel",)),
        name="qkv_proj",
    )(x2, wqvT, w_in, kbias)


def _proj_kernel(x_ref, w_ref, o_ref):
    o_ref[...] = jnp.dot(x_ref[...].astype(BF16), w_ref[...],
                         preferred_element_type=F32).astype(o_ref.dtype)


def _rest_proj(x2, w_in, layer, tm, tn):
    n, d = x2.shape
    gate_blocks = (w_in.shape[2] - COL_GATES) // tn
    first_gate = COL_GATES // tn
    first_rest = COL_REST // tn

    def w_block(i, j):
        return (layer, 0, jnp.where(j < gate_blocks, j + first_gate, j - gate_blocks + first_rest))

    return pl.pallas_call(
        _proj_kernel,
        grid=(n // tm, REST_COLS // tn),
        in_specs=[pl.BlockSpec((tm, d), lambda i, j: (i, 0)),
                  pl.BlockSpec((None, d, tn), w_block)],
        out_specs=pl.BlockSpec((tm, tn), lambda i, j: (i, j)),
        out_shape=jax.ShapeDtypeStruct((n, REST_COLS), F32),
        compiler_params=_cparams(("parallel", "arbitrary")),
        name="in_proj",
    )(x2, w_in)


def _attn_kernel(slope_ref, lam_ref, qT_ref, qTn_ref, ka_ref, vT_ref, dg_ref, w_ref, o_ref,
                 s_scr, cm_scr, m_scr, acc_scr, *, tile, layer, out_scale):
    h = pl.program_id(1)
    qi = pl.program_id(2)
    slope = slope_ref[h]
    lam = lam_ref[layer]

    m_scr[...] = jnp.full(m_scr.shape, NEG_INF, F32)
    acc_scr[...] = jnp.zeros(acc_scr.shape, F32)

    def scores(j, slot, q_ref=qT_ref):
        kt = ka_ref[0, 0, pl.ds(pl.multiple_of(j * tile, tile), tile), :]
        for mp in range(2):
            u = jnp.dot(kt, q_ref[0, 0, mp], preferred_element_type=F32)
            s_scr[slot, mp] = u
            cm_scr[slot, mp] = jnp.max(u, axis=0, keepdims=True)

    def update(mp, u, cm, shift, j):
        m_old = m_scr[mp]
        m_new = jnp.maximum(m_old, cm + shift)
        a = jnp.exp2(m_old - m_new)
        p = jnp.exp2(u - (m_new - shift))
        pv = jnp.dot(vT_ref[0, 0, j], p.astype(BF16), preferred_element_type=F32)
        acc_scr[mp] = a * acc_scr[mp] + pv
        m_scr[mp] = m_new

    @pl.when(qi == 0)
    def _():
        scores(0, 0)

    def past_tile(j, src, dst):
        shift = slope * ((j - qi) * tile).astype(F32)
        us = [s_scr[src, mp] for mp in range(2)]
        cms = [cm_scr[src, mp] for mp in range(2)]
        scores(j + 1, dst)
        for mp in range(2):
            update(mp, us[mp], cms[mp], shift, j)

    def past_group(jj, carry):
        for r in range(ATT_UNROLL):
            past_tile(ATT_UNROLL * jj + r, r % 2, (r + 1) % 2)
        return carry

    def past_single(j, carry):
        past_tile(j, 0, 0)
        return carry

    grouped = (qi // ATT_UNROLL) * ATT_UNROLL
    lax.fori_loop(0, qi // ATT_UNROLL, past_group, 0)
    lax.fori_loop(grouped, qi, past_single, 0)

    us = [s_scr[0, mp] + dg_ref[0] for mp in range(2)]
    scores(0, 0, qTn_ref)
    for mp in range(2):
        update(mp, us[mp], jnp.max(us[mp], axis=0, keepdims=True), jnp.float32(0.0), qi)

    o = (acc_scr[0, 0:DV] / acc_scr[0, DV:DV + 1]
         - lam * (acc_scr[1, 0:DV] / acc_scr[1, DV:DV + 1]))
    ms = jnp.mean(o * o, axis=0, keepdims=True)
    o = o * lax.rsqrt(ms + RMS_EPS)
    o = o * w_ref[...] * out_scale
    o_ref[0] = o.T


def _attention(slopes, lam, qT, ka, vT, diag, subln_col, layer, lambda_init):
    b, h, _, _, s = qT.shape
    t = ATT_TILE
    nt = s // t
    kern = functools.partial(_attn_kernel, tile=t, layer=layer, out_scale=1.0 - lambda_init)
    return pl.pallas_call(
        kern,
        grid=(b, h, nt),
        in_specs=[
            pl.BlockSpec(memory_space=pltpu.SMEM),
            pl.BlockSpec(memory_space=pltpu.SMEM),
            pl.BlockSpec((1, 1, 2, ATT_K, t), lambda bi, hi, qi: (bi, hi, 0, 0, qi)),
            pl.BlockSpec((1, 1, 2, ATT_K, t),
                         lambda bi, hi, qi: (bi, hi, 0, 0, jnp.minimum(qi + 1, nt - 1))),
            pl.BlockSpec((1, 1, s, ATT_K), lambda bi, hi, qi: (bi, hi, 0, 0)),
            pl.BlockSpec((1, 1, nt, DV_AUG, t), lambda bi, hi, qi: (bi, hi, 0, 0, 0)),
            pl.BlockSpec((1, t, t), lambda bi, hi, qi: (hi, 0, 0)),
            pl.BlockSpec((None, DV, 1), lambda bi, hi, qi: (layer, 0, 0)),
        ],
        out_specs=pl.BlockSpec((1, t, DV), lambda bi, hi, qi: (bi, qi, hi)),
        out_shape=jax.ShapeDtypeStruct((b, s, ATT_WIDTH), F32),
        scratch_shapes=[pltpu.VMEM((2, 2, t, t), F32),
                        pltpu.VMEM((2, 2, 1, t), F32),
                        pltpu.VMEM((2, 1, t), F32),
                        pltpu.VMEM((2, DV_AUG, t), F32)],
        compiler_params=_cparams(("arbitrary", "arbitrary", "arbitrary")),
        name="diff_attn",
    )(slopes, lam, qT, qT, ka, vT, diag, subln_col)


def _ssm_kernel(u_ref, toep_ref, qre_ref, qim_ref, pre_ref, pim_ref, are_ref, aim_ref,
                y_ref, ua_scr, ub_scr, up_scr, yp_scr, zre_scr, zim_scr, xre_scr, xim_scr,
                st_re, st_im, *, ncb):
    npb = SSM_PAIR_BLOCK
    pw = SSM_PAIR_W
    rows8 = 8 * SSM_T
    halves = (ua_scr, ub_scr)
    per_half = 128 // pw

    @pl.when(pl.program_id(2) == 0)
    def _():
        st_re[...] = jnp.zeros(st_re.shape, F32)
        st_im[...] = jnp.zeros(st_im.shape, F32)

    ua_scr[...] = u_ref[:, 0:128]
    ub_scr[...] = u_ref[:, 128:256]

    def gather(rb, carry):
        tok = pl.multiple_of(rb * rows8, rows8)
        row = pl.multiple_of(rb * 8, 8)
        for tau in range(SSM_T):
            for hf in range(2):
                ut = halves[hf][pl.ds(tok + tau, 8, stride=SSM_T), :]
                for q in range(per_half):
                    up_scr[hf * per_half + q, pl.ds(row, 8), tau * pw:(tau + 1) * pw] = (
                        ut[:, q * pw:(q + 1) * pw])
        return carry

    lax.fori_loop(0, ncb // 8, gather, 0)

    for pi in range(npb):
        up = up_scr[pi].astype(BF16)
        zre_scr[pl.ds(pi, ncb, stride=npb), :] = jnp.dot(
            up, qre_ref[pi], preferred_element_type=F32)
        zim_scr[pl.ds(pi, ncb, stride=npb), :] = jnp.dot(
            up, qim_ref[pi], preferred_element_type=F32)

    ar = are_ref[...]
    ai = aim_ref[...]

    def step(c, carry):
        xr, xi = carry
        row = pl.multiple_of(c * npb, npb)
        xre_scr[pl.ds(row, npb), :] = xr
        xim_scr[pl.ds(row, npb), :] = xi
        nxr = ar * xr - ai * xi + zre_scr[pl.ds(row, npb), :]
        nxi = ar * xi + ai * xr + zim_scr[pl.ds(row, npb), :]
        return nxr, nxi

    xr, xi = lax.fori_loop(0, ncb, step, (st_re[...], st_im[...]))
    st_re[...] = xr
    st_im[...] = xi

    for pi in range(npb):
        up = up_scr[pi].astype(BF16)
        xrp = xre_scr[pl.ds(pi, ncb, stride=npb), :].astype(BF16)
        xip = xim_scr[pl.ds(pi, ncb, stride=npb), :].astype(BF16)
        yp_scr[pi] = (jnp.dot(up, toep_ref[pi], preferred_element_type=F32)
                      + jnp.dot(xrp, pre_ref[pi], preferred_element_type=F32)
                      + jnp.dot(xip, pim_ref[pi], preferred_element_type=F32))

    def scatter(rb, carry):
        tok = pl.multiple_of(rb * rows8, rows8)
        row = pl.multiple_of(rb * 8, 8)
        for tau in range(SSM_T):
            for hf in range(2):
                yt = jnp.concatenate(
                    [yp_scr[hf * per_half + q, pl.ds(row, 8), tau * pw:(tau + 1) * pw]
                     for q in range(per_half)], axis=1)
                halves[hf][pl.ds(tok + tau, 8, stride=SSM_T), :] = yt
        return carry

    lax.fori_loop(0, ncb // 8, scatter, 0)
    y_ref[:, 0:128] = ua_scr[...]
    y_ref[:, 128:256] = ub_scr[...]


def _ssm_scan(rest, layer, b, s, toep, qre, qim, pre, pim, are, aim):
    n = rest.shape[0]
    npb = SSM_PAIR_BLOCK
    ncb = min(SSM_CHUNKS, s // SSM_T)
    tok = ncb * SSM_T
    cw = npb * SSM_PAIR_W
    nblk = s // tok
    u_col0 = R_U // cw
    w = SSM_ROW_W
    kern = functools.partial(_ssm_kernel, ncb=ncb)
    return pl.pallas_call(
        kern,
        grid=(b, SSM_PAIRS // npb, nblk),
        in_specs=[
            pl.BlockSpec((tok, cw), lambda bi, gi, ti: (bi * nblk + ti, u_col0 + gi)),
            pl.BlockSpec((None, npb, w, w), lambda bi, gi, ti: (layer, gi, 0, 0)),
            pl.BlockSpec((None, npb, w, 128), lambda bi, gi, ti: (layer, gi, 0, 0)),
            pl.BlockSpec((None, npb, w, 128), lambda bi, gi, ti: (layer, gi, 0, 0)),
            pl.BlockSpec((None, npb, 128, w), lambda bi, gi, ti: (layer, gi, 0, 0)),
            pl.BlockSpec((None, npb, 128, w), lambda bi, gi, ti: (layer, gi, 0, 0)),
            pl.BlockSpec((None, npb, 128), lambda bi, gi, ti: (layer, gi, 0)),
            pl.BlockSpec((None, npb, 128), lambda bi, gi, ti: (layer, gi, 0)),
        ],
        out_specs=pl.BlockSpec((tok, cw), lambda bi, gi, ti: (bi * nblk + ti, gi)),
        out_shape=jax.ShapeDtypeStruct((n, SSM_WIDTH), F32),
        scratch_shapes=[pltpu.VMEM((tok, 128), F32), pltpu.VMEM((tok, 128), F32),
                        pltpu.VMEM((npb, ncb, w), F32), pltpu.VMEM((npb, ncb, w), F32)]
        + [pltpu.VMEM((ncb * npb, 128), F32) for _ in range(4)]
        + [pltpu.VMEM((npb, 128), F32), pltpu.VMEM((npb, 128), F32)],
        compiler_params=_cparams(("parallel", "parallel", "arbitrary")),
        name="s5_scan",
    )(rest, toep, qre, qim, pre, pim, are, aim)


def _ssm_matrices(a_re, a_im, log_dt, b_re, b_im, c_re, c_im):
    hp = lax.Precision.HIGHEST
    g, p, t, hh = SSM_GROUPS, SSM_STATE, SSM_T, SSM_GROUP
    ar, ai = a_re.astype(F32), a_im.astype(F32)
    dt = jnp.exp(log_dt.astype(F32))[:, None]
    mag = jnp.exp(ar * dt)
    ph = ai * dt
    abr, abi = mag * jnp.cos(ph), mag * jnp.sin(ph)
    nr, ni = abr - 1.0, abi
    den = ar * ar + ai * ai
    fr = (nr * ar + ni * ai) / den
    fi = (ni * ar - nr * ai) / den
    br, bi = b_re.astype(F32), b_im.astype(F32)
    bbr = fr[..., None] * br - fi[..., None] * bi
    bbi = fr[..., None] * bi + fi[..., None] * br
    cr, ci = c_re.astype(F32), c_im.astype(F32)

    jj = jnp.arange(t + 1, dtype=F32)[:, None, None]
    mag_j = jnp.exp(jj * (ar * dt)[None])
    pwr, pwi = mag_j * jnp.cos(jj * ph[None]), mag_j * jnp.sin(jj * ph[None])

    cpr = cr[None] * pwr[:t, :, None, :] - ci[None] * pwi[:t, :, None, :]
    cpi = cr[None] * pwi[:t, :, None, :] + ci[None] * pwr[:t, :, None, :]
    kj = (jnp.einsum('jgap,gph->jgah', cpr, bbr, precision=hp)
          - jnp.einsum('jgap,gph->jgah', cpi, bbi, precision=hp))
    lag = jnp.arange(t)[None, :] - jnp.arange(t)[:, None]
    kk = kj[jnp.clip(lag, 0, t - 1)]
    kk = jnp.where((lag >= 0)[:, :, None, None, None], kk, 0.0)
    kk = kk.reshape(t, t, SSM_PAIRS, 2, hh, hh)
    eye2 = jnp.eye(2, dtype=F32)
    toep = jnp.einsum('abpgyx,gk->pagxbky', kk, eye2).reshape(SSM_PAIRS, SSM_ROW_W, SSM_ROW_W)

    rr, ri = pwr[t - 1::-1][:t], pwi[t - 1::-1][:t]
    qr = rr[..., None] * bbr[None] - ri[..., None] * bbi[None]
    qi = rr[..., None] * bbi[None] + ri[..., None] * bbr[None]

    def state_in(m):
        m = m.reshape(t, SSM_PAIRS, 2, p, hh)
        return jnp.einsum('apgsx,gk->pagxks', m, eye2).reshape(SSM_PAIRS, SSM_ROW_W, 2 * p)

    wr = cr[None] * pwr[1:, :, None, :] - ci[None] * pwi[1:, :, None, :]
    wi = cr[None] * pwi[1:, :, None, :] + ci[None] * pwr[1:, :, None, :]

    def state_out(m):
        m = m.reshape(t, SSM_PAIRS, 2, hh, p)
        return jnp.einsum('apgys,gk->pgsaky', m, eye2).reshape(SSM_PAIRS, 2 * p, SSM_ROW_W)

    are = pwr[t].reshape(SSM_PAIRS, 2 * p)
    aim = pwi[t].reshape(SSM_PAIRS, 2 * p)
    return (toep.astype(BF16), state_in(qr).astype(BF16), state_in(qi).astype(BF16),
            state_out(wr).astype(BF16), state_out(-wi).astype(BF16), are, aim)


def _merge_kernel(ya_ref, za_ref, ys_ref, u_ref, zs_ref, ga_ref, gs_ref,
                  d_ref, wglu_ref, wao_ref, wso_ref, o_ref):
    ya = ya_ref[...] * jax.nn.silu(za_ref[...])
    y = ys_ref[...] + d_ref[...] * u_ref[...]
    y = jax.nn.gelu(y)
    y = y * jax.nn.sigmoid(jnp.dot(y.astype(BF16), wglu_ref[...], preferred_element_type=F32))
    ys = y * jax.nn.silu(zs_ref[...])
    oa = jnp.dot(ya.astype(BF16), wao_ref[...], preferred_element_type=F32)
    os_ = jnp.dot(ys.astype(BF16), wso_ref[...], preferred_element_type=F32)
    merged = jax.nn.sigmoid(ga_ref[...]) * oa + jax.nn.sigmoid(gs_ref[...]) * os_
    o_ref[...] = merged.astype(o_ref.dtype)


def _merge(ya, ys, rest, d_row, wglu, wao, wso, layer, tm):
    n = ya.shape[0]
    d = wao.shape[2]
    w1 = ATT_WIDTH
    cb = lambda off, wd: pl.BlockSpec((tm, wd), lambda i: (i, off // wd))
    return pl.pallas_call(
        _merge_kernel,
        grid=(n // tm,),
        in_specs=[cb(0, w1), cb(R_ZATT, w1), cb(0, w1), cb(R_U, w1), cb(R_ZSSM, w1),
                  cb(R_GATT, d), cb(R_GSSM, d),
                  _resident(d_row.shape, layer), _resident(wglu.shape, layer),
                  _resident(wao.shape, layer), _resident(wso.shape, layer)],
        out_specs=pl.BlockSpec((tm, d), lambda i: (i, 0)),
        out_shape=jax.ShapeDtypeStruct((n, d), BF16),
        compiler_params=_cparams(("parallel",)),
        name="branch_merge",
    )(ya, rest, ys, rest, rest, rest, rest, d_row, wglu, wao, wso)


def _out_kernel(mg_ref, gp_ref, p_ref, x_ref, wout_ref, wple_ref, g_ref, b_ref, o_ref):
    mo = jnp.dot(mg_ref[...], wout_ref[...], preferred_element_type=F32)
    ple = jax.nn.sigmoid(gp_ref[...]) * jnp.dot(
        p_ref[...].astype(BF16), wple_ref[...], preferred_element_type=F32)
    r = ALPHA * x_ref[...] + mo + ple
    mu = jnp.mean(r, axis=-1, keepdims=True)
    rc = r - mu
    var = jnp.mean(rc * rc, axis=-1, keepdims=True)
    o_ref[...] = rc * lax.rsqrt(var + LN_EPS) * g_ref[...] + b_ref[...]


def _out_ln(merged, rest, p3, x2, wout, wple, g_row, b_row, layer, tm):
    n, d = x2.shape
    return pl.pallas_call(
        _out_kernel,
        grid=(n // tm,),
        in_specs=[pl.BlockSpec((tm, d), lambda i: (i, 0)),
                  pl.BlockSpec((tm, d), lambda i: (i, R_GPLE // d)),
                  pl.BlockSpec((None, tm, p3.shape[2]), lambda i: (layer, i, 0)),
                  pl.BlockSpec((tm, d), lambda i: (i, 0)),
                  _resident(wout.shape, layer), _resident(wple.shape, layer),
                  _resident(g_row.shape, layer), _resident(b_row.shape, layer)],
        out_specs=pl.BlockSpec((tm, d), lambda i: (i, 0)),
        out_shape=jax.ShapeDtypeStruct((n, d), F32),
        compiler_params=_cparams(("parallel",)),
        name="out_ln",
    )(merged, rest, p3, x2, wout, wple, g_row, b_row)


def _key_bias_columns(slopes):
    t = ATT_TILE
    v = slopes[:, None] * jnp.arange(t, dtype=F32)[None, :]
    parts = []
    for _ in range(BIAS_PARTS):
        top = lax.bitcast_convert_type(
            lax.bitcast_convert_type(v, jnp.uint32) & jnp.uint32(0xFFFF0000), F32)
        parts.append(top)
        v = v - top
    cols = jnp.stack(parts, axis=-1)
    pad = jnp.zeros((HEADS, t, ATT_K - 2 * DK - BIAS_PARTS), F32)
    return jnp.concatenate([cols, pad], axis=-1).astype(BF16)


def _diag_tile_terms(slopes):
    t = ATT_TILE
    ik = jnp.arange(t, dtype=jnp.int32)[:, None]
    iq = jnp.arange(t, dtype=jnp.int32)[None, :]
    ahead = jnp.maximum(ik - iq, 0).astype(F32)
    allowed = (ik // CHUNK) <= (iq // CHUNK)
    return jnp.where(allowed[None], -2.0 * slopes[:, None, None] * ahead[None], NEG_INF)


def kernel(x, p, w_in, w_att_out, w_ssm_out, w_out, w_ple, lambda_q1, lambda_k1, lambda_q2,
           lambda_k2, subln_w, ssm_a_re, ssm_a_im, ssm_log_dt, ssm_b_re, ssm_b_im, ssm_c_re,
           ssm_c_im, ssm_d, ssm_w_glu, ln_g, ln_b):
    b, s, d = x.shape
    n = b * s
    x2 = x.reshape(n, d)
    p3 = p.reshape(DEPTH, n, PLE_DIM)

    lambda_inits = [0.8 - 0.6 * math.exp(-0.3 * i) for i in range(DEPTH)]
    lam = (jnp.exp(jnp.sum(lambda_q1.astype(F32) * lambda_k1.astype(F32), axis=-1))
           - jnp.exp(jnp.sum(lambda_q2.astype(F32) * lambda_k2.astype(F32), axis=-1))
           + jnp.asarray(lambda_inits, F32))
    slopes = LOG2E * 2.0 ** (-8.0 * (jnp.arange(HEADS, dtype=F32) + 1.0) / HEADS)
    kbias = _key_bias_columns(slopes)
    diag = _diag_tile_terms(slopes)
    scale = LOG2E / math.sqrt(DK)
    wqvT = jnp.concatenate([w_in[:, :, COL_Q:COL_K] * scale, w_in[:, :, COL_V:COL_REST]],
                           axis=2).astype(BF16).transpose(0, 2, 1)
    w_in_b = w_in.astype(BF16)
    w_glu_b, w_ao_b, w_so_b = (ssm_w_glu.astype(BF16), w_att_out.astype(BF16),
                               w_ssm_out.astype(BF16))
    w_out_b, w_ple_b = w_out.astype(BF16), w_ple.astype(BF16)
    subln_col = subln_w.astype(F32).reshape(DEPTH, DV, 1)
    d_row = ssm_d.astype(F32).reshape(DEPTH, 1, SSM_WIDTH)
    g_row = ln_g.astype(F32).reshape(DEPTH, 1, d)
    b_row = ln_b.astype(F32).reshape(DEPTH, 1, d)
    mats = jax.vmap(_ssm_matrices)(ssm_a_re, ssm_a_im, ssm_log_dt, ssm_b_re, ssm_b_im,
                                   ssm_c_re, ssm_c_im)

    for i in range(DEPTH):
        qT, ka, vT = _qkv_proj(x2, wqvT, w_in_b, kbias, i, b, s)
        rest = _rest_proj(x2, w_in_b, i, 1024, 1024)
        y_att = _attention(slopes, lam, qT, ka, vT, diag, subln_col, i, lambda_inits[i])
        ys = _ssm_scan(rest, i, b, s, *mats)
        merged = _merge(y_att.reshape(n, ATT_WIDTH), ys, rest, d_row, w_glu_b, w_ao_b, w_so_b,
                        i, 256)
        x2 = _out_ln(merged, rest, p3, x2, w_out_b, w_ple_b, g_row, b_row, i, 256)
    return x2.reshape(b, s, d)
```

```python
import functools
import math

import jax
import jax.numpy as jnp
from jax import lax
from jax.experimental import pallas as pl
from jax.experimental.pallas import tpu as pltpu

F32 = jnp.float32
BF16 = jnp.bfloat16

DEPTH = 2
PLE_DIM = 256
CHUNK = 64
HEADS = 8
DK = 64
DV = 128
ATT_WIDTH = HEADS * DV
SSM_WIDTH = 1024
SSM_GROUP = 16
SSM_GROUPS = SSM_WIDTH // SSM_GROUP
SSM_STATE = 64
ALPHA = (2.0 * DEPTH) ** 0.25
LN_EPS = 1e-5
RMS_EPS = 1e-5
NEG_INF = -1e30

COL_Q, COL_K, COL_V, COL_REST = 0, 1024, 2048, 3072
COL_GATES = 6144
R_GATT, R_GSSM, R_GPLE, R_ZATT, R_U, R_ZSSM = 0, 2048, 4096, 6144, 7168, 8192
REST_COLS = 9216

ATT_TILE = 512
ATT_K = 256
ATT_UNROLL = 4
BIAS_PARTS = 3
DV_AUG = DV + 16
LOG2E = 1.4426950408889634
QKV_ROWS = 256
SSM_T = 16
SSM_PAIRS = SSM_GROUPS // 2
SSM_PAIR_W = 2 * SSM_GROUP
SSM_ROW_W = SSM_T * SSM_PAIR_W
SSM_PAIR_BLOCK = 8
SSM_CHUNKS = 128
VMEM_LIMIT = 48 * 1024 * 1024


def _cparams(sem):
    return pltpu.CompilerParams(dimension_semantics=sem, vmem_limit_bytes=VMEM_LIMIT)


def _resident(stacked_shape, layer, col_block=0):
    idx = (layer, 0, col_block)
    return pl.BlockSpec((None,) + tuple(stacked_shape[1:]), lambda *_: idx,
                        pipeline_mode=pl.Buffered(1))


def _qkv_kernel(x_ref, wqv_ref, wk_ref, kb_ref, qT_ref, ka_ref, vT_ref):
    xb = x_ref[...].astype(BF16)
    tm = xb.shape[0]
    nt = (((1,), (1,)), ((), ()))
    qv = lax.dot_general(wqv_ref[...], xb, nt, preferred_element_type=F32)
    k = jnp.dot(xb, wk_ref[...], preferred_element_type=F32)

    row = lax.broadcasted_iota(jnp.int32, (ATT_K - 2 * DK, tm), 0)
    ones_rows = jnp.where(row < BIAS_PARTS, 1.0, 0.0).astype(BF16)
    zeros = jnp.zeros((DK, tm), BF16)
    sum_rows = jnp.ones((DV_AUG - DV, tm), BF16)
    for h in range(HEADS):
        q1 = qv[(2 * h) * DK:(2 * h + 1) * DK].astype(BF16)
        q2 = qv[(2 * h + 1) * DK:(2 * h + 2) * DK].astype(BF16)
        qT_ref[0, h, 0] = jnp.concatenate([q1, zeros, ones_rows], axis=0)
        qT_ref[0, h, 1] = jnp.concatenate([zeros, q2, ones_rows], axis=0)
        ka_ref[0, h, :, 0:2 * DK] = k[:, h * 2 * DK:(h + 1) * 2 * DK].astype(BF16)
        ka_ref[0, h, :, 2 * DK:ATT_K] = kb_ref[h]
        v0 = HEADS * 2 * DK + h * DV
        vT_ref[0, h, 0] = jnp.concatenate([qv[v0:v0 + DV].astype(BF16), sum_rows], axis=0)


def _qkv_proj(x2, wqvT, w_in, kbias, layer, b, s):
    n, d = x2.shape
    tm = QKV_ROWS
    t = ATT_TILE
    per_b = s // tm
    sub = t // tm
    k_width = COL_V - COL_K
    return pl.pallas_call(
        _qkv_kernel,
        grid=(n // tm,),
        in_specs=[pl.BlockSpec((tm, d), lambda i: (i, 0)),
                  _resident(wqvT.shape, layer),
                  _resident((DEPTH, d, k_width), layer, COL_K // k_width),
                  pl.BlockSpec((HEADS, tm, ATT_K - 2 * DK), lambda i: (0, i % sub, 0))],
        out_specs=[
            pl.BlockSpec((1, HEADS, 2, ATT_K, tm), lambda i: (i // per_b, 0, 0, 0, i % per_b)),
            pl.BlockSpec((1, HEADS, tm, ATT_K), lambda i: (i // per_b, 0, i % per_b, 0)),
            pl.BlockSpec((1, HEADS, 1, DV_AUG, tm),
                         lambda i: (i // per_b, 0, (i % per_b) // sub, 0, i % sub)),
        ],
        out_shape=[jax.ShapeDtypeStruct((b, HEADS, 2, ATT_K, s), BF16),
                   jax.ShapeDtypeStruct((b, HEADS, s, ATT_K), BF16),
                   jax.ShapeDtypeStruct((b, HEADS, s // t, DV_AUG, t), BF16)],
        compiler_params=_cparams(("parallel",)),
        name="qkv_proj",
    )(x2, wqvT, w_in, kbias)


def _proj_kernel(x_ref, w_ref, o_ref):
    o_ref[...] = jnp.dot(x_ref[...].astype(BF16), w_ref[...],
                         preferred_element_type=F32).astype(o_ref.dtype)


def _rest_proj(x2, w_in, layer, tm, tn):
    n, d = x2.shape
    gate_blocks = (w_in.shape[2] - COL_GATES) // tn
    first_gate = COL_GATES // tn
    first_rest = COL_REST // tn

    def w_block(i, j):
        return (layer, 0, jnp.where(j < gate_blocks, j + first_gate, j - gate_blocks + first_rest))

    return pl.pallas_call(
        _proj_kernel,
        grid=(n // tm, REST_COLS // tn),
        in_specs=[pl.BlockSpec((tm, d), lambda i, j: (i, 0)),
                  pl.BlockSpec((None, d, tn), w_block)],
        out_specs=pl.BlockSpec((tm, tn), lambda i, j: (i, j)),
        out_shape=jax.ShapeDtypeStruct((n, REST_COLS), F32),
        compiler_params=_cparams(("parallel", "arbitrary")),
        name="in_proj",
    )(x2, w_in)


def _attn_kernel(slope_ref, lam_ref, qT_ref, qTn_ref, ka_ref, vT_ref, dg_ref, w_ref, o_ref,
                 s_scr, cm_scr, m_scr, acc_scr, *, tile, layer, out_scale):
    h = pl.program_id(1)
    qi = pl.program_id(2)
    slope = slope_ref[h]
    lam = lam_ref[layer]

    m_scr[...] = jnp.full(m_scr.shape, NEG_INF, F32)
    acc_scr[...] = jnp.zeros(acc_scr.shape, F32)

    def scores(j, slot, q_ref=qT_ref):
        kt = ka_ref[0, 0, pl.ds(pl.multiple_of(j * tile, tile), tile), :]
        for mp in range(2):
            u = jnp.dot(kt, q_ref[0, 0, mp], preferred_element_type=F32)
            s_scr[slot, mp] = u
            cm_scr[slot, mp] = jnp.max(u, axis=0, keepdims=True)

    def update(mp, u, cm, shift, j):
        m_old = m_scr[mp]
        m_new = jnp.maximum(m_old, cm + shift)
        a = jnp.exp2(m_old - m_new)
        p = jnp.exp2(u - (m_new - shift))
        pv = jnp.dot(vT_ref[0, 0, j], p.astype(BF16), preferred_element_type=F32)
        acc_scr[mp] = a * acc_scr[mp] + pv
        m_scr[mp] = m_new

    @pl.when(qi == 0)
    def _():
        scores(0, 0)

    def past_tile(j, src, dst):
        shift = slope * ((j - qi) * tile).astype(F32)
        us = [s_scr[src, mp] for mp in range(2)]
        cms = [cm_scr[src, mp] for mp in range(2)]
        scores(j + 1, dst)
        for mp in range(2):
            update(mp, us[mp], cms[mp], shift, j)

    def past_group(jj, carry):
        for r in range(ATT_UNROLL):
            past_tile(ATT_UNROLL * jj + r, r % 2, (r + 1) % 2)
        return carry

    def past_single(j, carry):
        past_tile(j, 0, 0)
        return carry

    grouped = (qi // ATT_UNROLL) * ATT_UNROLL
    lax.fori_loop(0, qi // ATT_UNROLL, past_group, 0)
    lax.fori_loop(grouped, qi, past_single, 0)

    us = [s_scr[0, mp] + dg_ref[0] for mp in range(2)]
    scores(0, 0, qTn_ref)
    for mp in range(2):
        update(mp, us[mp], jnp.max(us[mp], axis=0, keepdims=True), jnp.float32(0.0), qi)

    o = (acc_scr[0, 0:DV] / acc_scr[0, DV:DV + 1]
         - lam * (acc_scr[1, 0:DV] / acc_scr[1, DV:DV + 1]))
    ms = jnp.mean(o * o, axis=0, keepdims=True)
    o = o * lax.rsqrt(ms + RMS_EPS)
    o = o * w_ref[...] * out_scale
    o_ref[0] = o.T


def _attention(slopes, lam, qT, ka, vT, diag, subln_col, layer, lambda_init):
    b, h, _, _, s = qT.shape
    t = ATT_TILE
    nt = s // t
    kern = functools.partial(_attn_kernel, tile=t, layer=layer, out_scale=1.0 - lambda_init)
    return pl.pallas_call(
        kern,
        grid=(b, h, nt),
        in_specs=[
            pl.BlockSpec(memory_space=pltpu.SMEM),
            pl.BlockSpec(memory_space=pltpu.SMEM),
            pl.BlockSpec((1, 1, 2, ATT_K, t), lambda bi, hi, qi: (bi, hi, 0, 0, qi)),
            pl.BlockSpec((1, 1, 2, ATT_K, t),
                         lambda bi, hi, qi: (bi, hi, 0, 0, jnp.minimum(qi + 1, nt - 1))),
            pl.BlockSpec((1, 1, s, ATT_K), lambda bi, hi, qi: (bi, hi, 0, 0)),
            pl.BlockSpec((1, 1, nt, DV_AUG, t), lambda bi, hi, qi: (bi, hi, 0, 0, 0)),
            pl.BlockSpec((1, t, t), lambda bi, hi, qi: (hi, 0, 0)),
            pl.BlockSpec((None, DV, 1), lambda bi, hi, qi: (layer, 0, 0)),
        ],
        out_specs=pl.BlockSpec((1, t, DV), lambda bi, hi, qi: (bi, qi, hi)),
        out_shape=jax.ShapeDtypeStruct((b, s, ATT_WIDTH), F32),
        scratch_shapes=[pltpu.VMEM((2, 2, t, t), F32),
                        pltpu.VMEM((2, 2, 1, t), F32),
                        pltpu.VMEM((2, 1, t), F32),
                        pltpu.VMEM((2, DV_AUG, t), F32)],
        compiler_params=_cparams(("arbitrary", "arbitrary", "arbitrary")),
        name="diff_attn",
    )(slopes, lam, qT, qT, ka, vT, diag, subln_col)


def _ssm_kernel(u_ref, krow_ref, qre_ref, qim_ref, ptre_ref, ptim_ref, are_ref, aim_ref,
                y_ref, toep_scr, ua_scr, ub_scr, up_scr, yp_scr, zre_scr, zim_scr, xre_scr,
                xim_scr, st_re, st_im, *, ncb):
    npb = SSM_PAIR_BLOCK
    pw = SSM_PAIR_W
    w = SSM_ROW_W
    rows8 = 8 * SSM_T
    halves = (ua_scr, ub_scr)
    per_half = 128 // pw
    nt_dims = (((1,), (1,)), ((), ()))

    @pl.when(pl.program_id(2) == 0)
    def _():
        st_re[...] = jnp.zeros(st_re.shape, F32)
        st_im[...] = jnp.zeros(st_im.shape, F32)
        lane = lax.broadcasted_iota(jnp.int32, (pw, w), 1)
        for pi in range(npb):
            kr = krow_ref[pi]
            for tau in range(SSM_T):
                blk = kr if tau == 0 else jnp.where(
                    lane >= tau * pw, pltpu.roll(kr, tau * pw, axis=1), 0.0)
                toep_scr[pi, tau * pw:(tau + 1) * pw, :] = blk.astype(BF16)

    ua_scr[...] = u_ref[:, 0:128]
    ub_scr[...] = u_ref[:, 128:256]

    def gather(rb, carry):
        tok = pl.multiple_of(rb * rows8, rows8)
        row = pl.multiple_of(rb * 8, 8)
        for tau in range(SSM_T):
            for hf in range(2):
                ut = halves[hf][pl.ds(tok + tau, 8, stride=SSM_T), :]
                for q in range(per_half):
                    up_scr[hf * per_half + q, pl.ds(row, 8), tau * pw:(tau + 1) * pw] = (
                        ut[:, q * pw:(q + 1) * pw])
        return carry

    lax.fori_loop(0, ncb // 8, gather, 0)

    for pi in range(npb):
        up = up_scr[pi].astype(BF16)
        zre_scr[pl.ds(pi, ncb, stride=npb), :] = jnp.dot(
            up, qre_ref[:, pi].reshape(w, 128), preferred_element_type=F32)
        zim_scr[pl.ds(pi, ncb, stride=npb), :] = jnp.dot(
            up, qim_ref[:, pi].reshape(w, 128), preferred_element_type=F32)

    ar = are_ref[...]
    ai = aim_ref[...]

    def step(c, carry):
        xr, xi = carry
        row = pl.multiple_of(c * npb, npb)
        xre_scr[pl.ds(row, npb), :] = xr
        xim_scr[pl.ds(row, npb), :] = xi
        nxr = ar * xr - ai * xi + zre_scr[pl.ds(row, npb), :]
        nxi = ar * xi + ai * xr + zim_scr[pl.ds(row, npb), :]
        return nxr, nxi

    xr, xi = lax.fori_loop(0, ncb, step, (st_re[...], st_im[...]))
    st_re[...] = xr
    st_im[...] = xi

    for pi in range(npb):
        up = up_scr[pi].astype(BF16)
        xrp = xre_scr[pl.ds(pi, ncb, stride=npb), :].astype(BF16)
        xip = xim_scr[pl.ds(pi, ncb, stride=npb), :].astype(BF16)
        yp_scr[pi] = (
            jnp.dot(up, toep_scr[pi], preferred_element_type=F32)
            + lax.dot_general(xrp, ptre_ref[:, pi].reshape(w, 128), nt_dims,
                              preferred_element_type=F32)
            + lax.dot_general(xip, ptim_ref[:, pi].reshape(w, 128), nt_dims,
                              preferred_element_type=F32))

    def scatter(rb, carry):
        tok = pl.multiple_of(rb * rows8, rows8)
        row = pl.multiple_of(rb * 8, 8)
        for tau in range(SSM_T):
            for hf in range(2):
                yt = jnp.concatenate(
                    [yp_scr[hf * per_half + q, pl.ds(row, 8), tau * pw:(tau + 1) * pw]
                     for q in range(per_half)], axis=1)
                halves[hf][pl.ds(tok + tau, 8, stride=SSM_T), :] = yt
        return carry

    lax.fori_loop(0, ncb // 8, scatter, 0)
    y_ref[:, 0:128] = ua_scr[...]
    y_ref[:, 128:256] = ub_scr[...]


def _ssm_scan(rest, layer, b, s, krow, qre, qim, ptre, ptim, are, aim):
    n = rest.shape[0]
    npb = SSM_PAIR_BLOCK
    ncb = min(SSM_CHUNKS, s // SSM_T)
    tok = ncb * SSM_T
    cw = npb * SSM_PAIR_W
    nblk = s // tok
    u_col0 = R_U // cw
    w = SSM_ROW_W
    kern = functools.partial(_ssm_kernel, ncb=ncb)
    return pl.pallas_call(
        kern,
        grid=(b, SSM_PAIRS // npb, nblk),
        in_specs=[
            pl.BlockSpec((tok, cw), lambda bi, gi, ti: (bi * nblk + ti, u_col0 + gi)),
            pl.BlockSpec((None, npb, SSM_PAIR_W, w), lambda bi, gi, ti: (layer, gi, 0, 0)),
        ] + [pl.BlockSpec((None, SSM_T, npb, SSM_PAIR_W, 128),
                          lambda bi, gi, ti: (layer, 0, gi, 0, 0)) for _ in range(4)] + [
            pl.BlockSpec((None, npb, 128), lambda bi, gi, ti: (layer, gi, 0)),
            pl.BlockSpec((None, npb, 128), lambda bi, gi, ti: (layer, gi, 0)),
        ],
        out_specs=pl.BlockSpec((tok, cw), lambda bi, gi, ti: (bi * nblk + ti, gi)),
        out_shape=jax.ShapeDtypeStruct((n, SSM_WIDTH), F32),
        scratch_shapes=[pltpu.VMEM((npb, w, w), BF16),
                        pltpu.VMEM((tok, 128), F32), pltpu.VMEM((tok, 128), F32),
                        pltpu.VMEM((npb, ncb, w), F32), pltpu.VMEM((npb, ncb, w), F32)]
        + [pltpu.VMEM((ncb * npb, 128), F32) for _ in range(4)]
        + [pltpu.VMEM((npb, 128), F32), pltpu.VMEM((npb, 128), F32)],
        compiler_params=_cparams(("parallel", "parallel", "arbitrary")),
        name="s5_scan",
    )(rest, krow, qre, qim, ptre, ptim, are, aim)


def _ssm_matrices(a_re, a_im, log_dt, b_re, b_im, c_re, c_im):
    hp = lax.Precision.HIGHEST
    g, p, t, hh = SSM_GROUPS, SSM_STATE, SSM_T, SSM_GROUP
    ar, ai = a_re.astype(F32), a_im.astype(F32)
    dt = jnp.exp(log_dt.astype(F32))[:, None]
    mag = jnp.exp(ar * dt)
    ph = ai * dt
    abr, abi = mag * jnp.cos(ph), mag * jnp.sin(ph)
    nr, ni = abr - 1.0, abi
    den = ar * ar + ai * ai
    fr = (nr * ar + ni * ai) / den
    fi = (ni * ar - nr * ai) / den
    br, bi = b_re.astype(F32), b_im.astype(F32)
    bbr = fr[..., None] * br - fi[..., None] * bi
    bbi = fr[..., None] * bi + fi[..., None] * br
    cr, ci = c_re.astype(F32), c_im.astype(F32)

    jj = jnp.arange(t + 1, dtype=F32)[:, None, None]
    mag_j = jnp.exp(jj * (ar * dt)[None])
    pwr, pwi = mag_j * jnp.cos(jj * ph[None]), mag_j * jnp.sin(jj * ph[None])

    cpr = cr[None] * pwr[:t, :, None, :] - ci[None] * pwi[:t, :, None, :]
    cpi = cr[None] * pwi[:t, :, None, :] + ci[None] * pwr[:t, :, None, :]
    kj = (jnp.einsum('jgap,gph->jgah', cpr, bbr, precision=hp)
          - jnp.einsum('jgap,gph->jgah', cpi, bbi, precision=hp))
    eye2 = jnp.eye(2, dtype=F32)
    kj6 = kj.reshape(t, SSM_PAIRS, 2, hh, hh)
    krow = jnp.einsum('jpgyx,gk->pgxjky', kj6, eye2).reshape(SSM_PAIRS, SSM_PAIR_W, SSM_ROW_W)

    rr, ri = pwr[t - 1::-1][:t], pwi[t - 1::-1][:t]
    bbr_t, bbi_t = bbr.transpose(0, 2, 1), bbi.transpose(0, 2, 1)
    qr = rr[:, :, None, :] * bbr_t[None] - ri[:, :, None, :] * bbi_t[None]
    qi = rr[:, :, None, :] * bbi_t[None] + ri[:, :, None, :] * bbr_t[None]

    def state_in(m):
        m = m.reshape(t, SSM_PAIRS, 2, hh, p)
        return jnp.einsum('apgxs,gk->apgxks', m, eye2).reshape(t, SSM_PAIRS, SSM_PAIR_W, 2 * p)

    wr = cr[None] * pwr[1:, :, None, :] - ci[None] * pwi[1:, :, None, :]
    wi = cr[None] * pwi[1:, :, None, :] + ci[None] * pwr[1:, :, None, :]

    def state_out_t(m):
        m = m.reshape(t, SSM_PAIRS, 2, hh, p)
        return jnp.einsum('apkys,kg->apkygs', m, eye2).reshape(t, SSM_PAIRS, SSM_PAIR_W, 2 * p)

    are = pwr[t].reshape(SSM_PAIRS, 2 * p)
    aim = pwi[t].reshape(SSM_PAIRS, 2 * p)
    return (krow, state_in(qr).astype(BF16), state_in(qi).astype(BF16),
            state_out_t(wr).astype(BF16), state_out_t(-wi).astype(BF16), are, aim)


def _merge_kernel(ya_ref, za_ref, ys_ref, u_ref, zs_ref, ga_ref, gs_ref,
                  d_ref, wglu_ref, wao_ref, wso_ref, o_ref):
    ya = ya_ref[...] * jax.nn.silu(za_ref[...])
    y = ys_ref[...] + d_ref[...] * u_ref[...]
    y = jax.nn.gelu(y)
    y = y * jax.nn.sigmoid(jnp.dot(y.astype(BF16), wglu_ref[...], preferred_element_type=F32))
    ys = y * jax.nn.silu(zs_ref[...])
    oa = jnp.dot(ya.astype(BF16), wao_ref[...], preferred_element_type=F32)
    os_ = jnp.dot(ys.astype(BF16), wso_ref[...], preferred_element_type=F32)
    merged = jax.nn.sigmoid(ga_ref[...]) * oa + jax.nn.sigmoid(gs_ref[...]) * os_
    o_ref[...] = merged.astype(o_ref.dtype)


def _merge(ya, ys, rest, d_row, wglu, wao, wso, layer, tm):
    n = ya.shape[0]
    d = wao.shape[2]
    w1 = ATT_WIDTH
    cb = lambda off, wd: pl.BlockSpec((tm, wd), lambda i: (i, off // wd))
    return pl.pallas_call(
        _merge_kernel,
        grid=(n // tm,),
        in_specs=[cb(0, w1), cb(R_ZATT, w1), cb(0, w1), cb(R_U, w1), cb(R_ZSSM, w1),
                  cb(R_GATT, d), cb(R_GSSM, d),
                  _resident(d_row.shape, layer), _resident(wglu.shape, layer),
                  _resident(wao.shape, layer), _resident(wso.shape, layer)],
        out_specs=pl.BlockSpec((tm, d), lambda i: (i, 0)),
        out_shape=jax.ShapeDtypeStruct((n, d), BF16),
        compiler_params=_cparams(("parallel",)),
        name="branch_merge",
    )(ya, rest, ys, rest, rest, rest, rest, d_row, wglu, wao, wso)


def _out_kernel(mg_ref, gp_ref, p_ref, x_ref, wout_ref, wple_ref, g_ref, b_ref, o_ref):
    mo = jnp.dot(mg_ref[...], wout_ref[...], preferred_element_type=F32)
    ple = jax.nn.sigmoid(gp_ref[...]) * jnp.dot(
        p_ref[...].astype(BF16), wple_ref[...], preferred_element_type=F32)
    r = ALPHA * x_ref[...] + mo + ple
    mu = jnp.mean(r, axis=-1, keepdims=True)
    rc = r - mu
    var = jnp.mean(rc * rc, axis=-1, keepdims=True)
    o_ref[...] = rc * lax.rsqrt(var + LN_EPS) * g_ref[...] + b_ref[...]


def _out_ln(merged, rest, p3, x2, wout, wple, g_row, b_row, layer, tm):
    n, d = x2.shape
    return pl.pallas_call(
        _out_kernel,
        grid=(n // tm,),
        in_specs=[pl.BlockSpec((tm, d), lambda i: (i, 0)),
                  pl.BlockSpec((tm, d), lambda i: (i, R_GPLE // d)),
                  pl.BlockSpec((None, tm, p3.shape[2]), lambda i: (layer, i, 0)),
                  pl.BlockSpec((tm, d), lambda i: (i, 0)),
                  _resident(wout.shape, layer), _resident(wple.shape, layer),
                  _resident(g_row.shape, layer), _resident(b_row.shape, layer)],
        out_specs=pl.BlockSpec((tm, d), lambda i: (i, 0)),
        out_shape=jax.ShapeDtypeStruct((n, d), F32),
        compiler_params=_cparams(("parallel",)),
        name="out_ln",
    )(merged, rest, p3, x2, wout, wple, g_row, b_row)


def _key_bias_columns(slopes):
    t = ATT_TILE
    v = slopes[:, None] * jnp.arange(t, dtype=F32)[None, :]
    parts = []
    for _ in range(BIAS_PARTS):
        top = lax.bitcast_convert_type(
            lax.bitcast_convert_type(v, jnp.uint32) & jnp.uint32(0xFFFF0000), F32)
        parts.append(top)
        v = v - top
    cols = jnp.stack(parts, axis=-1)
    pad = jnp.zeros((HEADS, t, ATT_K - 2 * DK - BIAS_PARTS), F32)
    return jnp.concatenate([cols, pad], axis=-1).astype(BF16)


def _diag_tile_terms(slopes):
    t = ATT_TILE
    ik = jnp.arange(t, dtype=jnp.int32)[:, None]
    iq = jnp.arange(t, dtype=jnp.int32)[None, :]
    ahead = jnp.maximum(ik - iq, 0).astype(F32)
    allowed = (ik // CHUNK) <= (iq // CHUNK)
    return jnp.where(allowed[None], -2.0 * slopes[:, None, None] * ahead[None], NEG_INF)


def kernel(x, p, w_in, w_att_out, w_ssm_out, w_out, w_ple, lambda_q1, lambda_k1, lambda_q2,
           lambda_k2, subln_w, ssm_a_re, ssm_a_im, ssm_log_dt, ssm_b_re, ssm_b_im, ssm_c_re,
           ssm_c_im, ssm_d, ssm_w_glu, ln_g, ln_b):
    b, s, d = x.shape
    n = b * s
    x2 = x.reshape(n, d)
    p3 = p.reshape(DEPTH, n, PLE_DIM)

    lambda_inits = [0.8 - 0.6 * math.exp(-0.3 * i) for i in range(DEPTH)]
    lam = (jnp.exp(jnp.sum(lambda_q1.astype(F32) * lambda_k1.astype(F32), axis=-1))
           - jnp.exp(jnp.sum(lambda_q2.astype(F32) * lambda_k2.astype(F32), axis=-1))
           + jnp.asarray(lambda_inits, F32))
    slopes = LOG2E * 2.0 ** (-8.0 * (jnp.arange(HEADS, dtype=F32) + 1.0) / HEADS)
    kbias = _key_bias_columns(slopes)
    diag = _diag_tile_terms(slopes)
    scale = LOG2E / math.sqrt(DK)
    wqvT = jnp.concatenate([w_in[:, :, COL_Q:COL_K] * scale, w_in[:, :, COL_V:COL_REST]],
                           axis=2).astype(BF16).transpose(0, 2, 1)
    w_in_b = w_in.astype(BF16)
    w_glu_b, w_ao_b, w_so_b = (ssm_w_glu.astype(BF16), w_att_out.astype(BF16),
                               w_ssm_out.astype(BF16))
    w_out_b, w_ple_b = w_out.astype(BF16), w_ple.astype(BF16)
    subln_col = subln_w.astype(F32).reshape(DEPTH, DV, 1)
    d_row = ssm_d.astype(F32).reshape(DEPTH, 1, SSM_WIDTH)
    g_row = ln_g.astype(F32).reshape(DEPTH, 1, d)
    b_row = ln_b.astype(F32).reshape(DEPTH, 1, d)
    mats = jax.vmap(_ssm_matrices)(ssm_a_re, ssm_a_im, ssm_log_dt, ssm_b_re, ssm_b_im,
                                   ssm_c_re, ssm_c_im)

    for i in range(DEPTH):
        qT, ka, vT = _qkv_proj(x2, wqvT, w_in_b, kbias, i, b, s)
        rest = _rest_proj(x2, w_in_b, i, 1024, 1024)
        y_att = _attention(slopes, lam, qT, ka, vT, diag, subln_col, i, lambda_inits[i])
        ys = _ssm_scan(rest, i, b, s, *mats)
        merged = _merge(y_att.reshape(n, ATT_WIDTH), ys, rest, d_row, w_glu_b, w_ao_b, w_so_b,
                        i, 256)
        x2 = _out_ln(merged, rest, p3, x2, w_out_b, w_ple_b, g_row, b_row, i, 256)
    return x2.reshape(b, s, d)
```

```python
import functools
import math

import jax
import jax.numpy as jnp
from jax import lax
from jax.experimental import pallas as pl
from jax.experimental.pallas import tpu as pltpu

F32 = jnp.float32
BF16 = jnp.bfloat16

DEPTH = 2
PLE_DIM = 256
CHUNK = 64
HEADS = 8
DK = 64
DV = 128
ATT_WIDTH = HEADS * DV
SSM_WIDTH = 1024
SSM_GROUP = 16
SSM_GROUPS = SSM_WIDTH // SSM_GROUP
SSM_STATE = 64
ALPHA = (2.0 * DEPTH) ** 0.25
LN_EPS = 1e-5
RMS_EPS = 1e-5
NEG_INF = -1e30

COL_Q, COL_K, COL_V, COL_REST = 0, 1024, 2048, 3072
COL_GATES = 6144
R_GATT, R_GSSM, R_GPLE, R_ZATT, R_U, R_ZSSM = 0, 2048, 4096, 6144, 7168, 8192
REST_COLS = 9216

ATT_TILE = 512
ATT_K = 256
ATT_UNROLL = 4
BIAS_PARTS = 3
DV_AUG = DV + 16
LOG2E = 1.4426950408889634
QKV_ROWS = 256
SSM_T = 16
SSM_PAIRS = SSM_GROUPS // 2
SSM_PAIR_W = 2 * SSM_GROUP
SSM_ROW_W = SSM_T * SSM_PAIR_W
SSM_PAIR_BLOCK = 8
SSM_CHUNKS = 128
VMEM_LIMIT = 48 * 1024 * 1024


def _cparams(sem):
    return pltpu.CompilerParams(dimension_semantics=sem, vmem_limit_bytes=VMEM_LIMIT)


def _resident(stacked_shape, layer, col_block=0):
    idx = (layer, 0, col_block)
    return pl.BlockSpec((None,) + tuple(stacked_shape[1:]), lambda *_: idx,
                        pipeline_mode=pl.Buffered(1))


def _qkv_kernel(x_ref, wqv_ref, wk_ref, kb_ref, qT_ref, ka_ref, vT_ref):
    xb = x_ref[...].astype(BF16)
    tm = xb.shape[0]
    nt = (((1,), (1,)), ((), ()))
    qv = lax.dot_general(wqv_ref[...], xb, nt, preferred_element_type=F32)
    k = jnp.dot(xb, wk_ref[...], preferred_element_type=F32)

    row = lax.broadcasted_iota(jnp.int32, (ATT_K - 2 * DK, tm), 0)
    ones_rows = jnp.where(row < BIAS_PARTS, 1.0, 0.0).astype(BF16)
    zeros = jnp.zeros((DK, tm), BF16)
    sum_rows = jnp.ones((DV_AUG - DV, tm), BF16)
    for h in range(HEADS):
        q1 = qv[(2 * h) * DK:(2 * h + 1) * DK].astype(BF16)
        q2 = qv[(2 * h + 1) * DK:(2 * h + 2) * DK].astype(BF16)
        qT_ref[0, h, 0] = jnp.concatenate([q1, zeros, ones_rows], axis=0)
        qT_ref[0, h, 1] = jnp.concatenate([zeros, q2, ones_rows], axis=0)
        ka_ref[0, h, :, 0:2 * DK] = k[:, h * 2 * DK:(h + 1) * 2 * DK].astype(BF16)
        ka_ref[0, h, :, 2 * DK:ATT_K] = kb_ref[h]
        v0 = HEADS * 2 * DK + h * DV
        vT_ref[0, h, 0] = jnp.concatenate([qv[v0:v0 + DV].astype(BF16), sum_rows], axis=0)


def _qkv_proj(x2, wqvT, w_in, kbias, layer, b, s):
    n, d = x2.shape
    tm = QKV_ROWS
    t = ATT_TILE
    per_b = s // tm
    sub = t // tm
    k_width = COL_V - COL_K
    return pl.pallas_call(
        _qkv_kernel,
        grid=(n // tm,),
        in_specs=[pl.BlockSpec((tm, d), lambda i: (i, 0)),
                  _resident(wqvT.shape, layer),
                  _resident((DEPTH, d, k_width), layer, COL_K // k_width),
                  pl.BlockSpec((HEADS, tm, ATT_K - 2 * DK), lambda i: (0, i % sub, 0))],
        out_specs=[
            pl.BlockSpec((1, HEADS, 2, ATT_K, tm), lambda i: (i // per_b, 0, 0, 0, i % per_b)),
            pl.BlockSpec((1, HEADS, tm, ATT_K), lambda i: (i // per_b, 0, i % per_b, 0)),
            pl.BlockSpec((1, HEADS, 1, DV_AUG, tm),
                         lambda i: (i // per_b, 0, (i % per_b) // sub, 0, i % sub)),
        ],
        out_shape=[jax.ShapeDtypeStruct((b, HEADS, 2, ATT_K, s), BF16),
                   jax.ShapeDtypeStruct((b, HEADS, s, ATT_K), BF16),
                   jax.ShapeDtypeStruct((b, HEADS, s // t, DV_AUG, t), BF16)],
        compiler_params=_cparams(("parallel",)),
        name="qkv_proj",
    )(x2, wqvT, w_in, kbias)


def _proj_kernel(x_ref, w_ref, o_ref):
    o_ref[...] = jnp.dot(x_ref[...].astype(BF16), w_ref[...],
                         preferred_element_type=F32).astype(o_ref.dtype)


def _rest_proj(x2, w_in, layer, tm, tn):
    n, d = x2.shape
    gate_blocks = (w_in.shape[2] - COL_GATES) // tn
    first_gate = COL_GATES // tn
    first_rest = COL_REST // tn

    def w_block(i, j):
        return (layer, 0, jnp.where(j < gate_blocks, j + first_gate, j - gate_blocks + first_rest))

    return pl.pallas_call(
        _proj_kernel,
        grid=(n // tm, REST_COLS // tn),
        in_specs=[pl.BlockSpec((tm, d), lambda i, j: (i, 0)),
                  pl.BlockSpec((None, d, tn), w_block)],
        out_specs=pl.BlockSpec((tm, tn), lambda i, j: (i, j)),
        out_shape=jax.ShapeDtypeStruct((n, REST_COLS), F32),
        compiler_params=_cparams(("parallel", "arbitrary")),
        name="in_proj",
    )(x2, w_in)


def _attn_kernel(slope_ref, lam_ref, qT_ref, qTn_ref, ka_ref, vT_ref, dg_ref, w_ref, o_ref,
                 s_scr, cm_scr, m_scr, acc_scr, *, tile, layer, out_scale):
    h = pl.program_id(1)
    qi = pl.program_id(2)
    slope = slope_ref[h]
    lam = lam_ref[layer]

    m_scr[...] = jnp.full(m_scr.shape, NEG_INF, F32)
    acc_scr[...] = jnp.zeros(acc_scr.shape, F32)

    def scores(j, slot, q_ref=qT_ref):
        kt = ka_ref[0, 0, pl.ds(pl.multiple_of(j * tile, tile), tile), :]
        for mp in range(2):
            u = jnp.dot(kt, q_ref[0, 0, mp], preferred_element_type=F32)
            s_scr[slot, mp] = u
            cm_scr[slot, mp] = jnp.max(u, axis=0, keepdims=True)

    def update(mp, u, cm, shift, j):
        m_old = m_scr[mp]
        m_new = jnp.maximum(m_old, cm + shift)
        a = jnp.exp2(m_old - m_new)
        p = jnp.exp2(u - (m_new - shift))
        pv = jnp.dot(vT_ref[0, 0, j], p.astype(BF16), preferred_element_type=F32)
        acc_scr[mp] = a * acc_scr[mp] + pv
        m_scr[mp] = m_new

    @pl.when(qi == 0)
    def _():
        scores(0, 0)

    def past_tile(j, src, dst):
        shift = slope * ((j - qi) * tile).astype(F32)
        us = [s_scr[src, mp] for mp in range(2)]
        cms = [cm_scr[src, mp] for mp in range(2)]
        scores(j + 1, dst)
        for mp in range(2):
            update(mp, us[mp], cms[mp], shift, j)

    def past_group(jj, carry):
        for r in range(ATT_UNROLL):
            past_tile(ATT_UNROLL * jj + r, r % 2, (r + 1) % 2)
        return carry

    def past_single(j, carry):
        past_tile(j, 0, 0)
        return carry

    grouped = (qi // ATT_UNROLL) * ATT_UNROLL
    lax.fori_loop(0, qi // ATT_UNROLL, past_group, 0)
    lax.fori_loop(grouped, qi, past_single, 0)

    us = [s_scr[0, mp] + dg_ref[0] for mp in range(2)]
    scores(0, 0, qTn_ref)
    for mp in range(2):
        update(mp, us[mp], jnp.max(us[mp], axis=0, keepdims=True), jnp.float32(0.0), qi)

    o = (acc_scr[0, 0:DV] / acc_scr[0, DV:DV + 1]
         - lam * (acc_scr[1, 0:DV] / acc_scr[1, DV:DV + 1]))
    ms = jnp.mean(o * o, axis=0, keepdims=True)
    o = o * lax.rsqrt(ms + RMS_EPS)
    o = o * w_ref[...] * out_scale
    o_ref[0] = o.T


def _attention(slopes, lam, qT, ka, vT, diag, subln_col, layer, lambda_init):
    b, h, _, _, s = qT.shape
    t = ATT_TILE
    nt = s // t
    kern = functools.partial(_attn_kernel, tile=t, layer=layer, out_scale=1.0 - lambda_init)
    return pl.pallas_call(
        kern,
        grid=(b, h, nt),
        in_specs=[
            pl.BlockSpec(memory_space=pltpu.SMEM),
            pl.BlockSpec(memory_space=pltpu.SMEM),
            pl.BlockSpec((1, 1, 2, ATT_K, t), lambda bi, hi, qi: (bi, hi, 0, 0, qi)),
            pl.BlockSpec((1, 1, 2, ATT_K, t),
                         lambda bi, hi, qi: (bi, hi, 0, 0, jnp.minimum(qi + 1, nt - 1))),
            pl.BlockSpec((1, 1, s, ATT_K), lambda bi, hi, qi: (bi, hi, 0, 0)),
            pl.BlockSpec((1, 1, nt, DV_AUG, t), lambda bi, hi, qi: (bi, hi, 0, 0, 0)),
            pl.BlockSpec((1, t, t), lambda bi, hi, qi: (hi, 0, 0)),
            pl.BlockSpec((None, DV, 1), lambda bi, hi, qi: (layer, 0, 0)),
        ],
        out_specs=pl.BlockSpec((1, t, DV), lambda bi, hi, qi: (bi, qi, hi)),
        out_shape=jax.ShapeDtypeStruct((b, s, ATT_WIDTH), F32),
        scratch_shapes=[pltpu.VMEM((2, 2, t, t), F32),
                        pltpu.VMEM((2, 2, 1, t), F32),
                        pltpu.VMEM((2, 1, t), F32),
                        pltpu.VMEM((2, DV_AUG, t), F32)],
        compiler_params=_cparams(("arbitrary", "arbitrary", "arbitrary")),
        name="diff_attn",
    )(slopes, lam, qT, qT, ka, vT, diag, subln_col)


def _ssm_kernel(u_ref, krow_ref, qre_ref, qim_ref, ptre_ref, ptim_ref, are_ref, aim_ref,
                y_ref, toep_scr, ua_scr, ub_scr, up_scr, yp_scr, zre_scr, zim_scr, xre_scr,
                xim_scr, st_re, st_im, *, ncb):
    npb = SSM_PAIR_BLOCK
    pw = SSM_PAIR_W
    w = SSM_ROW_W
    rows8 = 8 * SSM_T
    halves = (ua_scr, ub_scr)
    per_half = 128 // pw
    nt_dims = (((1,), (1,)), ((), ()))

    @pl.when(pl.program_id(2) == 0)
    def _():
        st_re[...] = jnp.zeros(st_re.shape, F32)
        st_im[...] = jnp.zeros(st_im.shape, F32)
        lane = lax.broadcasted_iota(jnp.int32, (pw, w), 1)
        for pi in range(npb):
            kr = krow_ref[pi]
            for tau in range(SSM_T):
                blk = kr if tau == 0 else jnp.where(
                    lane >= tau * pw, pltpu.roll(kr, tau * pw, axis=1), 0.0)
                toep_scr[pi, tau * pw:(tau + 1) * pw, :] = blk.astype(BF16)

    ua_scr[...] = u_ref[:, 0:128]
    ub_scr[...] = u_ref[:, 128:256]

    def gather(rb, carry):
        tok = pl.multiple_of(rb * rows8, rows8)
        row = pl.multiple_of(rb * 8, 8)
        for tau in range(SSM_T):
            for hf in range(2):
                ut = halves[hf][pl.ds(tok + tau, 8, stride=SSM_T), :]
                for q in range(per_half):
                    up_scr[hf * per_half + q, pl.ds(row, 8), tau * pw:(tau + 1) * pw] = (
                        ut[:, q * pw:(q + 1) * pw])
        return carry

    lax.fori_loop(0, ncb // 8, gather, 0)

    for pi in range(npb):
        up = up_scr[pi].astype(BF16)
        zre_scr[pl.ds(pi, ncb, stride=npb), :] = jnp.dot(
            up, qre_ref[:, pi].reshape(w, 128), preferred_element_type=F32)
        zim_scr[pl.ds(pi, ncb, stride=npb), :] = jnp.dot(
            up, qim_ref[:, pi].reshape(w, 128), preferred_element_type=F32)

    ar = are_ref[...]
    ai = aim_ref[...]

    def step(c, carry):
        xr, xi = carry
        row = pl.multiple_of(c * npb, npb)
        xre_scr[pl.ds(row, npb), :] = xr
        xim_scr[pl.ds(row, npb), :] = xi
        nxr = ar * xr - ai * xi + zre_scr[pl.ds(row, npb), :]
        nxi = ar * xi + ai * xr + zim_scr[pl.ds(row, npb), :]
        return nxr, nxi

    xr, xi = lax.fori_loop(0, ncb, step, (st_re[...], st_im[...]))
    st_re[...] = xr
    st_im[...] = xi

    for pi in range(npb):
        up = up_scr[pi].astype(BF16)
        xrp = xre_scr[pl.ds(pi, ncb, stride=npb), :].astype(BF16)
        xip = xim_scr[pl.ds(pi, ncb, stride=npb), :].astype(BF16)
        yp_scr[pi] = (
            jnp.dot(up, toep_scr[pi], preferred_element_type=F32)
            + lax.dot_general(xrp, ptre_ref[:, pi].reshape(w, 128), nt_dims,
                              preferred_element_type=F32)
            + lax.dot_general(xip, ptim_ref[:, pi].reshape(w, 128), nt_dims,
                              preferred_element_type=F32))

    def scatter(rb, carry):
        tok = pl.multiple_of(rb * rows8, rows8)
        row = pl.multiple_of(rb * 8, 8)
        for tau in range(SSM_T):
            for hf in range(2):
                yt = jnp.concatenate(
                    [yp_scr[hf * per_half + q, pl.ds(row, 8), tau * pw:(tau + 1) * pw]
                     for q in range(per_half)], axis=1)
                halves[hf][pl.ds(tok + tau, 8, stride=SSM_T), :] = yt
        return carry

    lax.fori_loop(0, ncb // 8, scatter, 0)
    y_ref[:, 0:128] = ua_scr[...]
    y_ref[:, 128:256] = ub_scr[...]


def _ssm_scan(rest, layer, b, s, krow, qre, qim, ptre, ptim, are, aim):
    n = rest.shape[0]
    npb = SSM_PAIR_BLOCK
    ncb = min(SSM_CHUNKS, s // SSM_T)
    tok = ncb * SSM_T
    cw = npb * SSM_PAIR_W
    nblk = s // tok
    u_col0 = R_U // cw
    w = SSM_ROW_W
    kern = functools.partial(_ssm_kernel, ncb=ncb)
    return pl.pallas_call(
        kern,
        grid=(b, SSM_PAIRS // npb, nblk),
        in_specs=[
            pl.BlockSpec((tok, cw), lambda bi, gi, ti: (bi * nblk + ti, u_col0 + gi)),
            pl.BlockSpec((None, npb, SSM_PAIR_W, w), lambda bi, gi, ti: (layer, gi, 0, 0)),
        ] + [pl.BlockSpec((None, SSM_T, npb, SSM_PAIR_W, 128),
                          lambda bi, gi, ti: (layer, 0, gi, 0, 0)) for _ in range(4)] + [
            pl.BlockSpec((None, npb, 128), lambda bi, gi, ti: (layer, gi, 0)),
            pl.BlockSpec((None, npb, 128), lambda bi, gi, ti: (layer, gi, 0)),
        ],
        out_specs=pl.BlockSpec((tok, cw), lambda bi, gi, ti: (bi * nblk + ti, gi)),
        out_shape=jax.ShapeDtypeStruct((n, SSM_WIDTH), F32),
        scratch_shapes=[pltpu.VMEM((npb, w, w), BF16),
                        pltpu.VMEM((tok, 128), F32), pltpu.VMEM((tok, 128), F32),
                        pltpu.VMEM((npb, ncb, w), F32), pltpu.VMEM((npb, ncb, w), F32)]
        + [pltpu.VMEM((ncb * npb, 128), F32) for _ in range(4)]
        + [pltpu.VMEM((npb, 128), F32), pltpu.VMEM((npb, 128), F32)],
        compiler_params=_cparams(("parallel", "parallel", "arbitrary")),
        name="s5_scan",
    )(rest, krow, qre, qim, ptre, ptim, are, aim)


def _ssm_matrices(a_re, a_im, log_dt, b_re, b_im, c_re, c_im):
    hp = lax.Precision.HIGHEST
    g, p, t, hh = SSM_GROUPS, SSM_STATE, SSM_T, SSM_GROUP
    ar, ai = a_re.astype(F32), a_im.astype(F32)
    dt = jnp.exp(log_dt.astype(F32))[:, None]
    mag = jnp.exp(ar * dt)
    ph = ai * dt
    abr, abi = mag * jnp.cos(ph), mag * jnp.sin(ph)
    nr, ni = abr - 1.0, abi
    den = ar * ar + ai * ai
    fr = (nr * ar + ni * ai) / den
    fi = (ni * ar - nr * ai) / den
    br, bi = b_re.astype(F32), b_im.astype(F32)
    bbr = fr[..., None] * br - fi[..., None] * bi
    bbi = fr[..., None] * bi + fi[..., None] * br
    cr, ci = c_re.astype(F32), c_im.astype(F32)

    jj = jnp.arange(t + 1, dtype=F32)[:, None, None]
    mag_j = jnp.exp(jj * (ar * dt)[None])
    pwr, pwi = mag_j * jnp.cos(jj * ph[None]), mag_j * jnp.sin(jj * ph[None])

    cpr = cr[None] * pwr[:t, :, None, :] - ci[None] * pwi[:t, :, None, :]
    cpi = cr[None] * pwi[:t, :, None, :] + ci[None] * pwr[:t, :, None, :]
    kj = (jnp.einsum('jgap,gph->jgah', cpr, bbr, precision=hp)
          - jnp.einsum('jgap,gph->jgah', cpi, bbi, precision=hp))
    eye2 = jnp.eye(2, dtype=F32)
    kj6 = kj.reshape(t, SSM_PAIRS, 2, hh, hh)
    krow = jnp.einsum('jpgyx,gk->pgxjky', kj6, eye2).reshape(SSM_PAIRS, SSM_PAIR_W, SSM_ROW_W)

    rr, ri = pwr[t - 1::-1][:t], pwi[t - 1::-1][:t]
    bbr_t, bbi_t = bbr.transpose(0, 2, 1), bbi.transpose(0, 2, 1)
    qr = rr[:, :, None, :] * bbr_t[None] - ri[:, :, None, :] * bbi_t[None]
    qi = rr[:, :, None, :] * bbi_t[None] + ri[:, :, None, :] * bbr_t[None]

    wr = cr[None] * pwr[1:, :, None, :] - ci[None] * pwi[1:, :, None, :]
    wi = cr[None] * pwi[1:, :, None, :] + ci[None] * pwr[1:, :, None, :]

    def pair_block_diag(m):
        m = m.reshape(t, SSM_PAIRS, 2, hh, p)
        first = (jnp.arange(2) == 0)[None, None, :, None, None]
        wide = jnp.concatenate([jnp.where(first, m, 0.0), jnp.where(first, 0.0, m)], axis=-1)
        return wide.astype(BF16).reshape(t, SSM_PAIRS, SSM_PAIR_W, 2 * p)

    are = pwr[t].reshape(SSM_PAIRS, 2 * p)
    aim = pwi[t].reshape(SSM_PAIRS, 2 * p)
    return (krow, pair_block_diag(qr), pair_block_diag(qi),
            pair_block_diag(wr), pair_block_diag(-wi), are, aim)


def _merge_kernel(ya_ref, za_ref, ys_ref, u_ref, zs_ref, ga_ref, gs_ref,
                  d_ref, wglu_ref, wao_ref, wso_ref, o_ref):
    ya = ya_ref[...] * jax.nn.silu(za_ref[...])
    y = ys_ref[...] + d_ref[...] * u_ref[...]
    y = jax.nn.gelu(y)
    y = y * jax.nn.sigmoid(jnp.dot(y.astype(BF16), wglu_ref[...], preferred_element_type=F32))
    ys = y * jax.nn.silu(zs_ref[...])
    oa = jnp.dot(ya.astype(BF16), wao_ref[...], preferred_element_type=F32)
    os_ = jnp.dot(ys.astype(BF16), wso_ref[...], preferred_element_type=F32)
    merged = jax.nn.sigmoid(ga_ref[...]) * oa + jax.nn.sigmoid(gs_ref[...]) * os_
    o_ref[...] = merged.astype(o_ref.dtype)


def _merge(ya, ys, rest, d_row, wglu, wao, wso, layer, tm):
    n = ya.shape[0]
    d = wao.shape[2]
    w1 = ATT_WIDTH
    cb = lambda off, wd: pl.BlockSpec((tm, wd), lambda i: (i, off // wd))
    return pl.pallas_call(
        _merge_kernel,
        grid=(n // tm,),
        in_specs=[cb(0, w1), cb(R_ZATT, w1), cb(0, w1), cb(R_U, w1), cb(R_ZSSM, w1),
                  cb(R_GATT, d), cb(R_GSSM, d),
                  _resident(d_row.shape, layer), _resident(wglu.shape, layer),
                  _resident(wao.shape, layer), _resident(wso.shape, layer)],
        out_specs=pl.BlockSpec((tm, d), lambda i: (i, 0)),
        out_shape=jax.ShapeDtypeStruct((n, d), BF16),
        compiler_params=_cparams(("parallel",)),
        name="branch_merge",
    )(ya, rest, ys, rest, rest, rest, rest, d_row, wglu, wao, wso)


def _out_kernel(mg_ref, gp_ref, p_ref, x_ref, wout_ref, wple_ref, g_ref, b_ref, o_ref):
    mo = jnp.dot(mg_ref[...], wout_ref[...], preferred_element_type=F32)
    ple = jax.nn.sigmoid(gp_ref[...]) * jnp.dot(
        p_ref[...].astype(BF16), wple_ref[...], preferred_element_type=F32)
    r = ALPHA * x_ref[...] + mo + ple
    mu = jnp.mean(r, axis=-1, keepdims=True)
    rc = r - mu
    var = jnp.mean(rc * rc, axis=-1, keepdims=True)
    o_ref[...] = rc * lax.rsqrt(var + LN_EPS) * g_ref[...] + b_ref[...]


def _out_ln(merged, rest, p3, x2, wout, wple, g_row, b_row, layer, tm):
    n, d = x2.shape
    return pl.pallas_call(
        _out_kernel,
        grid=(n // tm,),
        in_specs=[pl.BlockSpec((tm, d), lambda i: (i, 0)),
                  pl.BlockSpec((tm, d), lambda i: (i, R_GPLE // d)),
                  pl.BlockSpec((None, tm, p3.shape[2]), lambda i: (layer, i, 0)),
                  pl.BlockSpec((tm, d), lambda i: (i, 0)),
                  _resident(wout.shape, layer), _resident(wple.shape, layer),
                  _resident(g_row.shape, layer), _resident(b_row.shape, layer)],
        out_specs=pl.BlockSpec((tm, d), lambda i: (i, 0)),
        out_shape=jax.ShapeDtypeStruct((n, d), F32),
        compiler_params=_cparams(("parallel",)),
        name="out_ln",
    )(merged, rest, p3, x2, wout, wple, g_row, b_row)


def _key_bias_columns(slopes):
    t = ATT_TILE
    v = slopes[:, None] * jnp.arange(t, dtype=F32)[None, :]
    parts = []
    for _ in range(BIAS_PARTS):
        top = lax.bitcast_convert_type(
            lax.bitcast_convert_type(v, jnp.uint32) & jnp.uint32(0xFFFF0000), F32)
        parts.append(top)
        v = v - top
    cols = jnp.stack(parts, axis=-1)
    pad = jnp.zeros((HEADS, t, ATT_K - 2 * DK - BIAS_PARTS), F32)
    return jnp.concatenate([cols, pad], axis=-1).astype(BF16)


def _diag_tile_terms(slopes):
    t = ATT_TILE
    ik = jnp.arange(t, dtype=jnp.int32)[:, None]
    iq = jnp.arange(t, dtype=jnp.int32)[None, :]
    ahead = jnp.maximum(ik - iq, 0).astype(F32)
    allowed = (ik // CHUNK) <= (iq // CHUNK)
    return jnp.where(allowed[None], -2.0 * slopes[:, None, None] * ahead[None], NEG_INF)


def kernel(x, p, w_in, w_att_out, w_ssm_out, w_out, w_ple, lambda_q1, lambda_k1, lambda_q2,
           lambda_k2, subln_w, ssm_a_re, ssm_a_im, ssm_log_dt, ssm_b_re, ssm_b_im, ssm_c_re,
           ssm_c_im, ssm_d, ssm_w_glu, ln_g, ln_b):
    b, s, d = x.shape
    n = b * s
    x2 = x.reshape(n, d)
    p3 = p.reshape(DEPTH, n, PLE_DIM)

    lambda_inits = [0.8 - 0.6 * math.exp(-0.3 * i) for i in range(DEPTH)]
    lam = (jnp.exp(jnp.sum(lambda_q1.astype(F32) * lambda_k1.astype(F32), axis=-1))
           - jnp.exp(jnp.sum(lambda_q2.astype(F32) * lambda_k2.astype(F32), axis=-1))
           + jnp.asarray(lambda_inits, F32))
    slopes = LOG2E * 2.0 ** (-8.0 * (jnp.arange(HEADS, dtype=F32) + 1.0) / HEADS)
    kbias = _key_bias_columns(slopes)
    diag = _diag_tile_terms(slopes)
    scale = LOG2E / math.sqrt(DK)
    wqvT = jnp.concatenate([w_in[:, :, COL_Q:COL_K] * scale, w_in[:, :, COL_V:COL_REST]],
                           axis=2).astype(BF16).transpose(0, 2, 1)
    w_in_b = w_in.astype(BF16)
    w_glu_b, w_ao_b, w_so_b = (ssm_w_glu.astype(BF16), w_att_out.astype(BF16),
                               w_ssm_out.astype(BF16))
    w_out_b, w_ple_b = w_out.astype(BF16), w_ple.astype(BF16)
    subln_col = subln_w.astype(F32).reshape(DEPTH, DV, 1)
    d_row = ssm_d.astype(F32).reshape(DEPTH, 1, SSM_WIDTH)
    g_row = ln_g.astype(F32).reshape(DEPTH, 1, d)
    b_row = ln_b.astype(F32).reshape(DEPTH, 1, d)
    mats = jax.vmap(_ssm_matrices)(ssm_a_re, ssm_a_im, ssm_log_dt, ssm_b_re, ssm_b_im,
                                   ssm_c_re, ssm_c_im)

    for i in range(DEPTH):
        qT, ka, vT = _qkv_proj(x2, wqvT, w_in_b, kbias, i, b, s)
        rest = _rest_proj(x2, w_in_b, i, 1024, 1024)
        y_att = _attention(slopes, lam, qT, ka, vT, diag, subln_col, i, lambda_inits[i])
        ys = _ssm_scan(rest, i, b, s, *mats)
        merged = _merge(y_att.reshape(n, ATT_WIDTH), ys, rest, d_row, w_glu_b, w_ao_b, w_so_b,
                        i, 256)
        x2 = _out_ln(merged, rest, p3, x2, w_out_b, w_ple_b, g_row, b_row, i, 256)
    return x2.reshape(b, s, d)
```

```python
import functools
import math

import jax
import jax.numpy as jnp
from jax import lax
from jax.experimental import pallas as pl
from jax.experimental.pallas import tpu as pltpu

F32 = jnp.float32
BF16 = jnp.bfloat16
LANES = 128

DEPTH = 2
PLE_DIM = 256
CHUNK = 64
HEADS = 8
DK = 64
DV = 128
ATT_WIDTH = HEADS * DV
SSM_WIDTH = 1024
SSM_GROUP = 16
SSM_GROUPS = SSM_WIDTH // SSM_GROUP
SSM_STATE = 64
ALPHA = (2.0 * DEPTH) ** 0.25
LN_EPS = 1e-5
RMS_EPS = 1e-5
NEG_INF = -1e30

COL_Q, COL_K, COL_V, COL_REST = 0, 1024, 2048, 3072
COL_GATES = 6144
R_GATT, R_GSSM, R_GPLE, R_ZATT, R_U, R_ZSSM = 0, 2048, 4096, 6144, 7168, 8192
REST_COLS = 9216

ATT_TILE = 512
ATT_K = 256
ATT_UNROLL = 4
BIAS_PARTS = 3
DV_AUG = DV + 16
LOG2E = 1.4426950408889634
QKV_ROWS = 256
SSM_T = 16
SSM_PAIRS = SSM_GROUPS // 2
SSM_PAIR_W = 2 * SSM_GROUP
SSM_ROW_W = SSM_T * SSM_PAIR_W
SSM_STATE_W = 2 * SSM_STATE
SSM_PAIR_BLOCK = 8
SSM_CHUNKS = 128
SSM_RELAYOUT_UNROLL = 4
VMEM_LIMIT = 48 * 1024 * 1024


def _cparams(sem):
    return pltpu.CompilerParams(dimension_semantics=sem, vmem_limit_bytes=VMEM_LIMIT)


def _resident(stacked_shape, layer, col_block=0):
    idx = (layer, 0, col_block)
    return pl.BlockSpec((None,) + tuple(stacked_shape[1:]), lambda *_: idx,
                        pipeline_mode=pl.Buffered(1))


def _qkv_kernel(x_ref, wqv_ref, wk_ref, kb_ref, qT_ref, ka_ref, vT_ref):
    xb = x_ref[...].astype(BF16)
    tm = xb.shape[0]
    nt = (((1,), (1,)), ((), ()))
    qv = lax.dot_general(wqv_ref[...], xb, nt, preferred_element_type=F32)
    k = jnp.dot(xb, wk_ref[...], preferred_element_type=F32)

    row = lax.broadcasted_iota(jnp.int32, (ATT_K - 2 * DK, tm), 0)
    ones_rows = jnp.where(row < BIAS_PARTS, 1.0, 0.0).astype(BF16)
    zeros = jnp.zeros((DK, tm), BF16)
    sum_rows = jnp.ones((DV_AUG - DV, tm), BF16)
    for h in range(HEADS):
        q1 = qv[(2 * h) * DK:(2 * h + 1) * DK].astype(BF16)
        q2 = qv[(2 * h + 1) * DK:(2 * h + 2) * DK].astype(BF16)
        qT_ref[0, h, 0] = jnp.concatenate([q1, zeros, ones_rows], axis=0)
        qT_ref[0, h, 1] = jnp.concatenate([zeros, q2, ones_rows], axis=0)
        ka_ref[0, h, :, 0:2 * DK] = k[:, h * 2 * DK:(h + 1) * 2 * DK].astype(BF16)
        ka_ref[0, h, :, 2 * DK:ATT_K] = kb_ref[h]
        v0 = HEADS * 2 * DK + h * DV
        vT_ref[0, h, 0] = jnp.concatenate([qv[v0:v0 + DV].astype(BF16), sum_rows], axis=0)


def _qkv_proj(x2, wqvT, w_in, kbias, layer, b, s):
    n, d = x2.shape
    tm = QKV_ROWS
    t = ATT_TILE
    per_b = s // tm
    sub = t // tm
    k_width = COL_V - COL_K
    return pl.pallas_call(
        _qkv_kernel,
        grid=(n // tm,),
        in_specs=[pl.BlockSpec((tm, d), lambda i: (i, 0)),
                  _resident(wqvT.shape, layer),
                  _resident((DEPTH, d, k_width), layer, COL_K // k_width),
                  pl.BlockSpec((HEADS, tm, ATT_K - 2 * DK), lambda i: (0, i % sub, 0))],
        out_specs=[
            pl.BlockSpec((1, HEADS, 2, ATT_K, tm), lambda i: (i // per_b, 0, 0, 0, i % per_b)),
            pl.BlockSpec((1, HEADS, tm, ATT_K), lambda i: (i // per_b, 0, i % per_b, 0)),
            pl.BlockSpec((1, HEADS, 1, DV_AUG, tm),
                         lambda i: (i // per_b, 0, (i % per_b) // sub, 0, i % sub)),
        ],
        out_shape=[jax.ShapeDtypeStruct((b, HEADS, 2, ATT_K, s), BF16),
                   jax.ShapeDtypeStruct((b, HEADS, s, ATT_K), BF16),
                   jax.ShapeDtypeStruct((b, HEADS, s // t, DV_AUG, t), BF16)],
        compiler_params=_cparams(("parallel",)),
        name="qkv_proj",
    )(x2, wqvT, w_in, kbias)


def _proj_kernel(x_ref, w_ref, o_ref):
    o_ref[...] = jnp.dot(x_ref[...].astype(BF16), w_ref[...],
                         preferred_element_type=F32).astype(o_ref.dtype)


def _rest_proj(x2, w_in, layer, tm, tn):
    n, d = x2.shape
    gate_blocks = (w_in.shape[2] - COL_GATES) // tn
    first_gate = COL_GATES // tn
    first_rest = COL_REST // tn

    def w_block(i, j):
        return (layer, 0, jnp.where(j < gate_blocks, j + first_gate, j - gate_blocks + first_rest))

    return pl.pallas_call(
        _proj_kernel,
        grid=(n // tm, REST_COLS // tn),
        in_specs=[pl.BlockSpec((tm, d), lambda i, j: (i, 0)),
                  pl.BlockSpec((None, d, tn), w_block)],
        out_specs=pl.BlockSpec((tm, tn), lambda i, j: (i, j)),
        out_shape=jax.ShapeDtypeStruct((n, REST_COLS), F32),
        compiler_params=_cparams(("parallel", "arbitrary")),
        name="in_proj",
    )(x2, w_in)


def _attn_kernel(slope_ref, lam_ref, qT_ref, qTn_ref, ka_ref, vT_ref, dg_ref, w_ref, o_ref,
                 s_scr, cm_scr, m_scr, acc_scr, *, tile, layer, out_scale):
    h = pl.program_id(1)
    qi = pl.program_id(2)
    slope = slope_ref[h]
    lam = lam_ref[layer]

    m_scr[...] = jnp.full(m_scr.shape, NEG_INF, F32)
    acc_scr[...] = jnp.zeros(acc_scr.shape, F32)

    def scores(j, slot, q_ref=qT_ref):
        kt = ka_ref[0, 0, pl.ds(pl.multiple_of(j * tile, tile), tile), :]
        for mp in range(2):
            u = jnp.dot(kt, q_ref[0, 0, mp], preferred_element_type=F32)
            s_scr[slot, mp] = u
            cm_scr[slot, mp] = jnp.max(u, axis=0, keepdims=True)

    def update(mp, u, cm, shift, j):
        m_old = m_scr[mp]
        m_new = jnp.maximum(m_old, cm + shift)
        a = jnp.exp2(m_old - m_new)
        p = jnp.exp2(u - (m_new - shift))
        pv = jnp.dot(vT_ref[0, 0, j], p.astype(BF16), preferred_element_type=F32)
        acc_scr[mp] = a * acc_scr[mp] + pv
        m_scr[mp] = m_new

    @pl.when(qi == 0)
    def _():
        scores(0, 0)

    def past_tile(j, src, dst):
        shift = slope * ((j - qi) * tile).astype(F32)
        us = [s_scr[src, mp] for mp in range(2)]
        cms = [cm_scr[src, mp] for mp in range(2)]
        scores(j + 1, dst)
        for mp in range(2):
            update(mp, us[mp], cms[mp], shift, j)

    def past_group(jj, carry):
        for r in range(ATT_UNROLL):
            past_tile(ATT_UNROLL * jj + r, r % 2, (r + 1) % 2)
        return carry

    def past_single(j, carry):
        past_tile(j, 0, 0)
        return carry

    grouped = (qi // ATT_UNROLL) * ATT_UNROLL
    lax.fori_loop(0, qi // ATT_UNROLL, past_group, 0)
    lax.fori_loop(grouped, qi, past_single, 0)

    us = [s_scr[0, mp] + dg_ref[0] for mp in range(2)]
    scores(0, 0, qTn_ref)
    for mp in range(2):
        update(mp, us[mp], jnp.max(us[mp], axis=0, keepdims=True), jnp.float32(0.0), qi)

    o = (acc_scr[0, 0:DV] / acc_scr[0, DV:DV + 1]
         - lam * (acc_scr[1, 0:DV] / acc_scr[1, DV:DV + 1]))
    ms = jnp.mean(o * o, axis=0, keepdims=True)
    o = o * lax.rsqrt(ms + RMS_EPS)
    o = o * w_ref[...] * out_scale
    o_ref[0] = o.T


def _attention(slopes, lam, qT, ka, vT, diag, subln_col, layer, lambda_init):
    b, h, _, _, s = qT.shape
    t = ATT_TILE
    nt = s // t
    kern = functools.partial(_attn_kernel, tile=t, layer=layer, out_scale=1.0 - lambda_init)
    return pl.pallas_call(
        kern,
        grid=(b, h, nt),
        in_specs=[
            pl.BlockSpec(memory_space=pltpu.SMEM),
            pl.BlockSpec(memory_space=pltpu.SMEM),
            pl.BlockSpec((1, 1, 2, ATT_K, t), lambda bi, hi, qi: (bi, hi, 0, 0, qi)),
            pl.BlockSpec((1, 1, 2, ATT_K, t),
                         lambda bi, hi, qi: (bi, hi, 0, 0, jnp.minimum(qi + 1, nt - 1))),
            pl.BlockSpec((1, 1, s, ATT_K), lambda bi, hi, qi: (bi, hi, 0, 0)),
            pl.BlockSpec((1, 1, nt, DV_AUG, t), lambda bi, hi, qi: (bi, hi, 0, 0, 0)),
            pl.BlockSpec((1, t, t), lambda bi, hi, qi: (hi, 0, 0)),
            pl.BlockSpec((None, DV, 1), lambda bi, hi, qi: (layer, 0, 0)),
        ],
        out_specs=pl.BlockSpec((1, t, DV), lambda bi, hi, qi: (bi, qi, hi)),
        out_shape=jax.ShapeDtypeStruct((b, s, ATT_WIDTH), F32),
        scratch_shapes=[pltpu.VMEM((2, 2, t, t), F32),
                        pltpu.VMEM((2, 2, 1, t), F32),
                        pltpu.VMEM((2, 1, t), F32),
                        pltpu.VMEM((2, DV_AUG, t), F32)],
        compiler_params=_cparams(("arbitrary", "arbitrary", "arbitrary")),
        name="diff_attn",
    )(slopes, lam, qT, qT, ka, vT, diag, subln_col)


def _ssm_kernel(u_ref, krow_ref, qre_ref, qim_ref, ptre_ref, ptim_ref, are_ref, aim_ref,
                y_ref, toep_scr, ua_scr, ub_scr, up_scr, yp_scr, zre_scr, zim_scr, xre_scr,
                xim_scr, st_re, st_im, *, ncb):
    npb = SSM_PAIR_BLOCK
    pw = SSM_PAIR_W
    w = SSM_ROW_W
    rows8 = 8 * SSM_T
    halves = (ua_scr, ub_scr)
    per_half = LANES // pw
    nt_dims = (((1,), (1,)), ((), ()))

    @pl.when(pl.program_id(2) == 0)
    def _():
        st_re[...] = jnp.zeros(st_re.shape, F32)
        st_im[...] = jnp.zeros(st_im.shape, F32)
        lane = lax.broadcasted_iota(jnp.int32, (pw, w), 1)
        for pi in range(npb):
            kr = krow_ref[pi]
            for tau in range(SSM_T):
                blk = kr if tau == 0 else jnp.where(
                    lane >= tau * pw, pltpu.roll(kr, tau * pw, axis=1), 0.0)
                toep_scr[pi, tau * pw:(tau + 1) * pw, :] = blk.astype(BF16)

    ua_scr[...] = u_ref[:, 0:LANES]
    ub_scr[...] = u_ref[:, LANES:2 * LANES]

    def gather(rb, carry):
        tok = pl.multiple_of(rb * rows8, rows8)
        row = pl.multiple_of(rb * 8, 8)
        for tau in range(SSM_T):
            for hf in range(2):
                ut = halves[hf][pl.ds(tok + tau, 8, stride=SSM_T), :]
                for q in range(per_half):
                    up_scr[hf * per_half + q, pl.ds(row, 8), tau * pw:(tau + 1) * pw] = (
                        ut[:, q * pw:(q + 1) * pw])
        return carry

    lax.fori_loop(0, ncb // 8, gather, 0, unroll=SSM_RELAYOUT_UNROLL)

    for pi in range(npb):
        up = up_scr[pi].astype(BF16)
        zre_scr[pl.ds(pi, ncb, stride=npb), :] = jnp.dot(
            up, qre_ref[:, pi].reshape(w, SSM_STATE_W), preferred_element_type=F32)
        zim_scr[pl.ds(pi, ncb, stride=npb), :] = jnp.dot(
            up, qim_ref[:, pi].reshape(w, SSM_STATE_W), preferred_element_type=F32)

    ar = are_ref[...]
    ai = aim_ref[...]

    def step(c, carry):
        xr, xi = carry
        row = pl.multiple_of(c * npb, npb)
        xre_scr[pl.ds(row, npb), :] = xr
        xim_scr[pl.ds(row, npb), :] = xi
        nxr = ar * xr - ai * xi + zre_scr[pl.ds(row, npb), :]
        nxi = ar * xi + ai * xr + zim_scr[pl.ds(row, npb), :]
        return nxr, nxi

    xr, xi = lax.fori_loop(0, ncb, step, (st_re[...], st_im[...]))
    st_re[...] = xr
    st_im[...] = xi

    for pi in range(npb):
        up = up_scr[pi].astype(BF16)
        xrp = xre_scr[pl.ds(pi, ncb, stride=npb), :].astype(BF16)
        xip = xim_scr[pl.ds(pi, ncb, stride=npb), :].astype(BF16)
        yp_scr[pi] = (
            jnp.dot(up, toep_scr[pi], preferred_element_type=F32)
            + lax.dot_general(xrp, ptre_ref[:, pi].reshape(w, SSM_STATE_W), nt_dims,
                              preferred_element_type=F32)
            + lax.dot_general(xip, ptim_ref[:, pi].reshape(w, SSM_STATE_W), nt_dims,
                              preferred_element_type=F32))

    def scatter(rb, carry):
        tok = pl.multiple_of(rb * rows8, rows8)
        row = pl.multiple_of(rb * 8, 8)
        for tau in range(SSM_T):
            for hf in range(2):
                yt = jnp.concatenate(
                    [yp_scr[hf * per_half + q, pl.ds(row, 8), tau * pw:(tau + 1) * pw]
                     for q in range(per_half)], axis=1)
                halves[hf][pl.ds(tok + tau, 8, stride=SSM_T), :] = yt
        return carry

    lax.fori_loop(0, ncb // 8, scatter, 0, unroll=SSM_RELAYOUT_UNROLL)
    y_ref[:, 0:LANES] = ua_scr[...]
    y_ref[:, LANES:2 * LANES] = ub_scr[...]


def _ssm_scan(rest, layer, b, s, krow, qre, qim, ptre, ptim, are, aim):
    n = rest.shape[0]
    npb = SSM_PAIR_BLOCK
    ncb = min(SSM_CHUNKS, s // SSM_T)
    tok = ncb * SSM_T
    cw = npb * SSM_PAIR_W
    nblk = s // tok
    u_col0 = R_U // cw
    w = SSM_ROW_W
    kern = functools.partial(_ssm_kernel, ncb=ncb)
    return pl.pallas_call(
        kern,
        grid=(b, SSM_PAIRS // npb, nblk),
        in_specs=[
            pl.BlockSpec((tok, cw), lambda bi, gi, ti: (bi * nblk + ti, u_col0 + gi)),
            pl.BlockSpec((None, npb, SSM_PAIR_W, w), lambda bi, gi, ti: (layer, gi, 0, 0)),
        ] + [pl.BlockSpec((None, SSM_T, npb, SSM_PAIR_W, SSM_STATE_W),
                          lambda bi, gi, ti: (layer, 0, gi, 0, 0)) for _ in range(4)] + [
            pl.BlockSpec((None, npb, SSM_STATE_W), lambda bi, gi, ti: (layer, gi, 0)),
            pl.BlockSpec((None, npb, SSM_STATE_W), lambda bi, gi, ti: (layer, gi, 0)),
        ],
        out_specs=pl.BlockSpec((tok, cw), lambda bi, gi, ti: (bi * nblk + ti, gi)),
        out_shape=jax.ShapeDtypeStruct((n, SSM_WIDTH), F32),
        scratch_shapes=[pltpu.VMEM((npb, w, w), BF16),
                        pltpu.VMEM((tok, LANES), F32), pltpu.VMEM((tok, LANES), F32),
                        pltpu.VMEM((npb, ncb, w), F32), pltpu.VMEM((npb, ncb, w), F32)]
        + [pltpu.VMEM((ncb * npb, SSM_STATE_W), F32) for _ in range(4)]
        + [pltpu.VMEM((npb, SSM_STATE_W), F32), pltpu.VMEM((npb, SSM_STATE_W), F32)],
        compiler_params=_cparams(("parallel", "parallel", "arbitrary")),
        name="s5_scan",
    )(rest, krow, qre, qim, ptre, ptim, are, aim)


def _ssm_matrices(a_re, a_im, log_dt, b_re, b_im, c_re, c_im):
    hp = lax.Precision.HIGHEST
    g, p, t, hh = SSM_GROUPS, SSM_STATE, SSM_T, SSM_GROUP
    ar, ai = a_re.astype(F32), a_im.astype(F32)
    dt = jnp.exp(log_dt.astype(F32))[:, None]
    mag = jnp.exp(ar * dt)
    ph = ai * dt
    abr, abi = mag * jnp.cos(ph), mag * jnp.sin(ph)
    nr, ni = abr - 1.0, abi
    den = ar * ar + ai * ai
    fr = (nr * ar + ni * ai) / den
    fi = (ni * ar - nr * ai) / den
    br, bi = b_re.astype(F32), b_im.astype(F32)
    bbr = fr[..., None] * br - fi[..., None] * bi
    bbi = fr[..., None] * bi + fi[..., None] * br
    cr, ci = c_re.astype(F32), c_im.astype(F32)

    jj = jnp.arange(t + 1, dtype=F32)[:, None, None]
    mag_j = jnp.exp(jj * (ar * dt)[None])
    pwr, pwi = mag_j * jnp.cos(jj * ph[None]), mag_j * jnp.sin(jj * ph[None])

    cpr = cr[None] * pwr[:t, :, None, :] - ci[None] * pwi[:t, :, None, :]
    cpi = cr[None] * pwi[:t, :, None, :] + ci[None] * pwr[:t, :, None, :]
    kj = (jnp.einsum('jgap,gph->jgah', cpr, bbr, precision=hp)
          - jnp.einsum('jgap,gph->jgah', cpi, bbi, precision=hp))
    eye2 = jnp.eye(2, dtype=F32)
    kj6 = kj.reshape(t, SSM_PAIRS, 2, hh, hh)
    krow = jnp.einsum('jpgyx,gk->pgxjky', kj6, eye2).reshape(SSM_PAIRS, SSM_PAIR_W, SSM_ROW_W)

    rr, ri = pwr[t - 1::-1][:t], pwi[t - 1::-1][:t]
    bbr_t, bbi_t = bbr.transpose(0, 2, 1), bbi.transpose(0, 2, 1)
    qr = rr[:, :, None, :] * bbr_t[None] - ri[:, :, None, :] * bbi_t[None]
    qi = rr[:, :, None, :] * bbi_t[None] + ri[:, :, None, :] * bbr_t[None]

    wr = cr[None] * pwr[1:, :, None, :] - ci[None] * pwi[1:, :, None, :]
    wi = cr[None] * pwi[1:, :, None, :] + ci[None] * pwr[1:, :, None, :]

    def pair_block_diag(m):
        m = m.reshape(t, SSM_PAIRS, 2, hh, p)
        first = (jnp.arange(2) == 0)[None, None, :, None, None]
        wide = jnp.concatenate([jnp.where(first, m, 0.0), jnp.where(first, 0.0, m)], axis=-1)
        return wide.astype(BF16).reshape(t, SSM_PAIRS, SSM_PAIR_W, 2 * p)

    are = pwr[t].reshape(SSM_PAIRS, 2 * p)
    aim = pwi[t].reshape(SSM_PAIRS, 2 * p)
    return (krow, pair_block_diag(qr), pair_block_diag(qi),
            pair_block_diag(wr), pair_block_diag(-wi), are, aim)


def _merge_kernel(ya_ref, za_ref, ys_ref, u_ref, zs_ref, ga_ref, gs_ref,
                  d_ref, wglu_ref, wao_ref, wso_ref, o_ref):
    ya = ya_ref[...] * jax.nn.silu(za_ref[...])
    y = ys_ref[...] + d_ref[...] * u_ref[...]
    y = jax.nn.gelu(y)
    y = y * jax.nn.sigmoid(jnp.dot(y.astype(BF16), wglu_ref[...], preferred_element_type=F32))
    ys = y * jax.nn.silu(zs_ref[...])
    oa = jnp.dot(ya.astype(BF16), wao_ref[...], preferred_element_type=F32)
    os_ = jnp.dot(ys.astype(BF16), wso_ref[...], preferred_element_type=F32)
    merged = jax.nn.sigmoid(ga_ref[...]) * oa + jax.nn.sigmoid(gs_ref[...]) * os_
    o_ref[...] = merged.astype(o_ref.dtype)


def _merge(ya, ys, rest, d_row, wglu, wao, wso, layer, tm):
    n = ya.shape[0]
    d = wao.shape[2]
    w1 = ATT_WIDTH
    cb = lambda off, wd: pl.BlockSpec((tm, wd), lambda i: (i, off // wd))
    return pl.pallas_call(
        _merge_kernel,
        grid=(n // tm,),
        in_specs=[cb(0, w1), cb(R_ZATT, w1), cb(0, w1), cb(R_U, w1), cb(R_ZSSM, w1),
                  cb(R_GATT, d), cb(R_GSSM, d),
                  _resident(d_row.shape, layer), _resident(wglu.shape, layer),
                  _resident(wao.shape, layer), _resident(wso.shape, layer)],
        out_specs=pl.BlockSpec((tm, d), lambda i: (i, 0)),
        out_shape=jax.ShapeDtypeStruct((n, d), BF16),
        compiler_params=_cparams(("parallel",)),
        name="branch_merge",
    )(ya, rest, ys, rest, rest, rest, rest, d_row, wglu, wao, wso)


def _out_kernel(mg_ref, gp_ref, p_ref, x_ref, wout_ref, wple_ref, g_ref, b_ref, o_ref):
    mo = jnp.dot(mg_ref[...], wout_ref[...], preferred_element_type=F32)
    ple = jax.nn.sigmoid(gp_ref[...]) * jnp.dot(
        p_ref[...].astype(BF16), wple_ref[...], preferred_element_type=F32)
    r = ALPHA * x_ref[...] + mo + ple
    mu = jnp.mean(r, axis=-1, keepdims=True)
    rc = r - mu
    var = jnp.mean(rc * rc, axis=-1, keepdims=True)
    o_ref[...] = rc * lax.rsqrt(var + LN_EPS) * g_ref[...] + b_ref[...]


def _out_ln(merged, rest, p3, x2, wout, wple, g_row, b_row, layer, tm):
    n, d = x2.shape
    return pl.pallas_call(
        _out_kernel,
        grid=(n // tm,),
        in_specs=[pl.BlockSpec((tm, d), lambda i: (i, 0)),
                  pl.BlockSpec((tm, d), lambda i: (i, R_GPLE // d)),
                  pl.BlockSpec((None, tm, p3.shape[2]), lambda i: (layer, i, 0)),
                  pl.BlockSpec((tm, d), lambda i: (i, 0)),
                  _resident(wout.shape, layer), _resident(wple.shape, layer),
                  _resident(g_row.shape, layer), _resident(b_row.shape, layer)],
        out_specs=pl.BlockSpec((tm, d), lambda i: (i, 0)),
        out_shape=jax.ShapeDtypeStruct((n, d), F32),
        compiler_params=_cparams(("parallel",)),
        name="out_ln",
    )(merged, rest, p3, x2, wout, wple, g_row, b_row)


def _key_bias_columns(slopes):
    t = ATT_TILE
    v = slopes[:, None] * jnp.arange(t, dtype=F32)[None, :]
    parts = []
    for _ in range(BIAS_PARTS):
        top = lax.bitcast_convert_type(
            lax.bitcast_convert_type(v, jnp.uint32) & jnp.uint32(0xFFFF0000), F32)
        parts.append(top)
        v = v - top
    cols = jnp.stack(parts, axis=-1)
    pad = jnp.zeros((HEADS, t, ATT_K - 2 * DK - BIAS_PARTS), F32)
    return jnp.concatenate([cols, pad], axis=-1).astype(BF16)


def _diag_tile_terms(slopes):
    t = ATT_TILE
    ik = jnp.arange(t, dtype=jnp.int32)[:, None]
    iq = jnp.arange(t, dtype=jnp.int32)[None, :]
    ahead = jnp.maximum(ik - iq, 0).astype(F32)
    allowed = (ik // CHUNK) <= (iq // CHUNK)
    return jnp.where(allowed[None], -2.0 * slopes[:, None, None] * ahead[None], NEG_INF)


def kernel(x, p, w_in, w_att_out, w_ssm_out, w_out, w_ple, lambda_q1, lambda_k1, lambda_q2,
           lambda_k2, subln_w, ssm_a_re, ssm_a_im, ssm_log_dt, ssm_b_re, ssm_b_im, ssm_c_re,
           ssm_c_im, ssm_d, ssm_w_glu, ln_g, ln_b):
    b, s, d = x.shape
    n = b * s
    x2 = x.reshape(n, d)
    p3 = p.reshape(DEPTH, n, PLE_DIM)

    lambda_inits = [0.8 - 0.6 * math.exp(-0.3 * i) for i in range(DEPTH)]
    lam = (jnp.exp(jnp.sum(lambda_q1.astype(F32) * lambda_k1.astype(F32), axis=-1))
           - jnp.exp(jnp.sum(lambda_q2.astype(F32) * lambda_k2.astype(F32), axis=-1))
           + jnp.asarray(lambda_inits, F32))
    slopes = LOG2E * 2.0 ** (-8.0 * (jnp.arange(HEADS, dtype=F32) + 1.0) / HEADS)
    kbias = _key_bias_columns(slopes)
    diag = _diag_tile_terms(slopes)
    scale = LOG2E / math.sqrt(DK)
    wqvT = jnp.concatenate([w_in[:, :, COL_Q:COL_K] * scale, w_in[:, :, COL_V:COL_REST]],
                           axis=2).astype(BF16).transpose(0, 2, 1)
    w_in_b = w_in.astype(BF16)
    w_glu_b, w_ao_b, w_so_b = (ssm_w_glu.astype(BF16), w_att_out.astype(BF16),
                               w_ssm_out.astype(BF16))
    w_out_b, w_ple_b = w_out.astype(BF16), w_ple.astype(BF16)
    subln_col = subln_w.astype(F32).reshape(DEPTH, DV, 1)
    d_row = ssm_d.astype(F32).reshape(DEPTH, 1, SSM_WIDTH)
    g_row = ln_g.astype(F32).reshape(DEPTH, 1, d)
    b_row = ln_b.astype(F32).reshape(DEPTH, 1, d)
    mats = jax.vmap(_ssm_matrices)(ssm_a_re, ssm_a_im, ssm_log_dt, ssm_b_re, ssm_b_im,
                                   ssm_c_re, ssm_c_im)

    for i in range(DEPTH):
        qT, ka, vT = _qkv_proj(x2, wqvT, w_in_b, kbias, i, b, s)
        rest = _rest_proj(x2, w_in_b, i, 1024, 1024)
        y_att = _attention(slopes, lam, qT, ka, vT, diag, subln_col, i, lambda_inits[i])
        ys = _ssm_scan(rest, i, b, s, *mats)
        merged = _merge(y_att.reshape(n, ATT_WIDTH), ys, rest, d_row, w_glu_b, w_ao_b, w_so_b,
                        i, 256)
        x2 = _out_ln(merged, rest, p3, x2, w_out_b, w_ple_b, g_row, b_row, i, 256)
    return x2.reshape(b, s, d)
```

```python
import functools
import math

import jax
import jax.numpy as jnp
from jax import lax
from jax.experimental import pallas as pl
from jax.experimental.pallas import tpu as pltpu

F32 = jnp.float32
BF16 = jnp.bfloat16
LANES = 128

DEPTH = 2
PLE_DIM = 256
CHUNK = 64
HEADS = 8
DK = 64
DV = 128
ATT_WIDTH = HEADS * DV
SSM_WIDTH = 1024
SSM_GROUP = 16
SSM_GROUPS = SSM_WIDTH // SSM_GROUP
SSM_STATE = 64
ALPHA = (2.0 * DEPTH) ** 0.25
LN_EPS = 1e-5
RMS_EPS = 1e-5
NEG_INF = -1e30

COL_Q, COL_K, COL_V, COL_REST = 0, 1024, 2048, 3072
COL_GATES = 6144
R_GATT, R_GSSM, R_GPLE, R_ZATT, R_U, R_ZSSM = 0, 2048, 4096, 6144, 7168, 8192
REST_COLS = 9216

ATT_TILE = 512
ATT_K = 256
ATT_GROUPS = (8, 4)
BIAS_PARTS = 3
DV_AUG = DV + 16
LOG2E = 1.4426950408889634
QKV_ROWS = 256
WQV_BLOCK = 512
SSM_T = 16
SSM_PAIRS = SSM_GROUPS // 2
SSM_PAIR_W = 2 * SSM_GROUP
SSM_ROW_W = SSM_T * SSM_PAIR_W
SSM_STATE_W = 2 * SSM_STATE
SSM_PAIR_BLOCK = 8
SSM_CHUNKS = 128
SSM_RELAYOUT_UNROLL = 4
VMEM_LIMIT = 48 * 1024 * 1024


def _cparams(sem):
    return pltpu.CompilerParams(dimension_semantics=sem, vmem_limit_bytes=VMEM_LIMIT)


def _resident(stacked_shape, layer, col_block=0):
    idx = (layer, 0, col_block)
    return pl.BlockSpec((None,) + tuple(stacked_shape[1:]), lambda *_: idx,
                        pipeline_mode=pl.Buffered(1))


def _wqv_kernel(w_ref, o_ref, *, q_blocks, q_scale):
    scale = jnp.where(pl.program_id(1) < q_blocks, q_scale, 1.0)
    o_ref[...] = (w_ref[...] * scale).T.astype(o_ref.dtype)


def _wqv_transposed(w_in, q_scale):
    depth, d, _ = w_in.shape
    blk = WQV_BLOCK
    q_blocks = (COL_K - COL_Q) // blk
    v_first = COL_V // blk
    out_rows = (COL_K - COL_Q) + (COL_REST - COL_V)
    kern = functools.partial(_wqv_kernel, q_blocks=q_blocks, q_scale=q_scale)
    return pl.pallas_call(
        kern,
        grid=(depth, out_rows // blk, d // blk),
        in_specs=[pl.BlockSpec(
            (None, blk, blk),
            lambda l, c, r: (l, r, jnp.where(c < q_blocks, c, c - q_blocks + v_first)))],
        out_specs=pl.BlockSpec((None, blk, blk), lambda l, c, r: (l, c, r)),
        out_shape=jax.ShapeDtypeStruct((depth, out_rows, d), BF16),
        compiler_params=_cparams(("parallel", "parallel", "parallel")),
        name="wqv_transpose",
    )(w_in)


def _qkv_kernel(x_ref, wqv_ref, wk_ref, kb_ref, qT_ref, ka_ref, vT_ref):
    xb = x_ref[...].astype(BF16)
    tm = xb.shape[0]
    nt = (((1,), (1,)), ((), ()))
    qv = lax.dot_general(wqv_ref[...], xb, nt, preferred_element_type=F32)
    k = jnp.dot(xb, wk_ref[...], preferred_element_type=F32)

    row = lax.broadcasted_iota(jnp.int32, (ATT_K - 2 * DK, tm), 0)
    ones_rows = jnp.where(row < BIAS_PARTS, 1.0, 0.0).astype(BF16)
    zeros = jnp.zeros((DK, tm), BF16)
    sum_rows = jnp.ones((DV_AUG - DV, tm), BF16)
    for h in range(HEADS):
        q1 = qv[(2 * h) * DK:(2 * h + 1) * DK].astype(BF16)
        q2 = qv[(2 * h + 1) * DK:(2 * h + 2) * DK].astype(BF16)
        qT_ref[0, h, 0] = jnp.concatenate([q1, zeros, ones_rows], axis=0)
        qT_ref[0, h, 1] = jnp.concatenate([zeros, q2, ones_rows], axis=0)
        ka_ref[0, h, :, 0:2 * DK] = k[:, h * 2 * DK:(h + 1) * 2 * DK].astype(BF16)
        ka_ref[0, h, :, 2 * DK:ATT_K] = kb_ref[h]
        v0 = HEADS * 2 * DK + h * DV
        vT_ref[0, h, 0] = jnp.concatenate([qv[v0:v0 + DV].astype(BF16), sum_rows], axis=0)


def _qkv_proj(x2, wqvT, w_in, kbias, layer, b, s):
    n, d = x2.shape
    tm = QKV_ROWS
    t = ATT_TILE
    per_b = s // tm
    sub = t // tm
    k_width = COL_V - COL_K
    return pl.pallas_call(
        _qkv_kernel,
        grid=(n // tm,),
        in_specs=[pl.BlockSpec((tm, d), lambda i: (i, 0)),
                  _resident(wqvT.shape, layer),
                  _resident((DEPTH, d, k_width), layer, COL_K // k_width),
                  pl.BlockSpec((HEADS, tm, ATT_K - 2 * DK), lambda i: (0, i % sub, 0))],
        out_specs=[
            pl.BlockSpec((1, HEADS, 2, ATT_K, tm), lambda i: (i // per_b, 0, 0, 0, i % per_b)),
            pl.BlockSpec((1, HEADS, tm, ATT_K), lambda i: (i // per_b, 0, i % per_b, 0)),
            pl.BlockSpec((1, HEADS, 1, DV_AUG, tm),
                         lambda i: (i // per_b, 0, (i % per_b) // sub, 0, i % sub)),
        ],
        out_shape=[jax.ShapeDtypeStruct((b, HEADS, 2, ATT_K, s), BF16),
                   jax.ShapeDtypeStruct((b, HEADS, s, ATT_K), BF16),
                   jax.ShapeDtypeStruct((b, HEADS, s // t, DV_AUG, t), BF16)],
        compiler_params=_cparams(("parallel",)),
        name="qkv_proj",
    )(x2, wqvT, w_in, kbias)


def _proj_kernel(x_ref, w_ref, o_ref):
    o_ref[...] = jnp.dot(x_ref[...].astype(BF16), w_ref[...],
                         preferred_element_type=F32).astype(o_ref.dtype)


def _rest_proj(x2, w_in, layer, tm, tn):
    n, d = x2.shape
    gate_blocks = (w_in.shape[2] - COL_GATES) // tn
    first_gate = COL_GATES // tn
    first_rest = COL_REST // tn

    def w_block(i, j):
        return (layer, 0, jnp.where(j < gate_blocks, j + first_gate, j - gate_blocks + first_rest))

    return pl.pallas_call(
        _proj_kernel,
        grid=(n // tm, REST_COLS // tn),
        in_specs=[pl.BlockSpec((tm, d), lambda i, j: (i, 0)),
                  pl.BlockSpec((None, d, tn), w_block)],
        out_specs=pl.BlockSpec((tm, tn), lambda i, j: (i, j)),
        out_shape=jax.ShapeDtypeStruct((n, REST_COLS), F32),
        compiler_params=_cparams(("parallel", "arbitrary")),
        name="in_proj",
    )(x2, w_in)


def _attn_kernel(slope_ref, lam_ref, qT_ref, qTn_ref, ka_ref, vT_ref, dg_ref, w_ref, o_ref,
                 s_scr, cm_scr, m_scr, acc_scr, *, tile, layer, out_scale):
    h = pl.program_id(1)
    qi = pl.program_id(2)
    slope = slope_ref[h]
    lam = lam_ref[layer]

    m_scr[...] = jnp.full(m_scr.shape, NEG_INF, F32)
    acc_scr[...] = jnp.zeros(acc_scr.shape, F32)

    def scores(j, slot, q_ref=qT_ref):
        kt = ka_ref[0, 0, pl.ds(pl.multiple_of(j * tile, tile), tile), :]
        for mp in range(2):
            u = jnp.dot(kt, q_ref[0, 0, mp], preferred_element_type=F32)
            s_scr[slot, mp] = u
            cm_scr[slot, mp] = jnp.max(u, axis=0, keepdims=True)

    def update(mp, u, cm, shift, j):
        m_old = m_scr[mp]
        m_new = jnp.maximum(m_old, cm + shift)
        a = jnp.exp2(m_old - m_new)
        p = jnp.exp2(u - (m_new - shift))
        pv = jnp.dot(vT_ref[0, 0, j], p.astype(BF16), preferred_element_type=F32)
        acc_scr[mp] = a * acc_scr[mp] + pv
        m_scr[mp] = m_new

    @pl.when(qi == 0)
    def _():
        scores(0, 0)

    def past_tile(j, src, dst):
        shift = slope * ((j - qi) * tile).astype(F32)
        us = [s_scr[src, mp] for mp in range(2)]
        cms = [cm_scr[src, mp] for mp in range(2)]
        scores(j + 1, dst)
        for mp in range(2):
            update(mp, us[mp], cms[mp], shift, j)

    def past_single(j, carry):
        past_tile(j, 0, 0)
        return carry

    done = 0
    for size in ATT_GROUPS:
        def past_group(jj, carry, size=size, base=done):
            for r in range(size):
                past_tile(base + size * jj + r, r % 2, (r + 1) % 2)
            return carry

        trips = (qi - done) // size
        lax.fori_loop(0, trips, past_group, 0)
        done = done + trips * size
    lax.fori_loop(done, qi, past_single, 0)

    us = [s_scr[0, mp] + dg_ref[0] for mp in range(2)]
    scores(0, 0, qTn_ref)
    for mp in range(2):
        update(mp, us[mp], jnp.max(us[mp], axis=0, keepdims=True), jnp.float32(0.0), qi)

    o = (acc_scr[0, 0:DV] / acc_scr[0, DV:DV + 1]
         - lam * (acc_scr[1, 0:DV] / acc_scr[1, DV:DV + 1]))
    ms = jnp.mean(o * o, axis=0, keepdims=True)
    o = o * lax.rsqrt(ms + RMS_EPS)
    o = o * w_ref[...] * out_scale
    o_ref[0] = o.T


def _attention(slopes, lam, qT, ka, vT, diag, subln_col, layer, lambda_init):
    b, h, _, _, s = qT.shape
    t = ATT_TILE
    nt = s // t
    kern = functools.partial(_attn_kernel, tile=t, layer=layer, out_scale=1.0 - lambda_init)
    return pl.pallas_call(
        kern,
        grid=(b, h, nt),
        in_specs=[
            pl.BlockSpec(memory_space=pltpu.SMEM),
            pl.BlockSpec(memory_space=pltpu.SMEM),
            pl.BlockSpec((1, 1, 2, ATT_K, t), lambda bi, hi, qi: (bi, hi, 0, 0, qi)),
            pl.BlockSpec((1, 1, 2, ATT_K, t),
                         lambda bi, hi, qi: (bi, hi, 0, 0, jnp.minimum(qi + 1, nt - 1))),
            pl.BlockSpec((1, 1, s, ATT_K), lambda bi, hi, qi: (bi, hi, 0, 0)),
            pl.BlockSpec((1, 1, nt, DV_AUG, t), lambda bi, hi, qi: (bi, hi, 0, 0, 0)),
            pl.BlockSpec((1, t, t), lambda bi, hi, qi: (hi, 0, 0)),
            pl.BlockSpec((None, DV, 1), lambda bi, hi, qi: (layer, 0, 0)),
        ],
        out_specs=pl.BlockSpec((1, t, DV), lambda bi, hi, qi: (bi, qi, hi)),
        out_shape=jax.ShapeDtypeStruct((b, s, ATT_WIDTH), F32),
        scratch_shapes=[pltpu.VMEM((2, 2, t, t), F32),
                        pltpu.VMEM((2, 2, 1, t), F32),
                        pltpu.VMEM((2, 1, t), F32),
                        pltpu.VMEM((2, DV_AUG, t), F32)],
        compiler_params=_cparams(("arbitrary", "arbitrary", "arbitrary")),
        name="diff_attn",
    )(slopes, lam, qT, qT, ka, vT, diag, subln_col)


def _ssm_kernel(u_ref, krow_ref, qre_ref, qim_ref, ptre_ref, ptim_ref, are_ref, aim_ref,
                y_ref, toep_scr, ua_scr, ub_scr, up_scr, yp_scr, zre_scr, zim_scr, xre_scr,
                xim_scr, st_re, st_im, *, ncb):
    npb = SSM_PAIR_BLOCK
    pw = SSM_PAIR_W
    w = SSM_ROW_W
    rows8 = 8 * SSM_T
    halves = (ua_scr, ub_scr)
    per_half = LANES // pw
    nt_dims = (((1,), (1,)), ((), ()))

    @pl.when(pl.program_id(2) == 0)
    def _():
        st_re[...] = jnp.zeros(st_re.shape, F32)
        st_im[...] = jnp.zeros(st_im.shape, F32)
        lane = lax.broadcasted_iota(jnp.int32, (pw, w), 1)
        for pi in range(npb):
            kr = krow_ref[pi]
            for tau in range(SSM_T):
                blk = kr if tau == 0 else jnp.where(
                    lane >= tau * pw, pltpu.roll(kr, tau * pw, axis=1), 0.0)
                toep_scr[pi, tau * pw:(tau + 1) * pw, :] = blk.astype(BF16)

    ua_scr[...] = u_ref[:, 0:LANES]
    ub_scr[...] = u_ref[:, LANES:2 * LANES]

    def gather(rb, carry):
        tok = pl.multiple_of(rb * rows8, rows8)
        row = pl.multiple_of(rb * 8, 8)
        for tau in range(SSM_T):
            for hf in range(2):
                ut = halves[hf][pl.ds(tok + tau, 8, stride=SSM_T), :]
                for q in range(per_half):
                    up_scr[hf * per_half + q, pl.ds(row, 8), tau * pw:(tau + 1) * pw] = (
                        ut[:, q * pw:(q + 1) * pw])
        return carry

    lax.fori_loop(0, ncb // 8, gather, 0, unroll=SSM_RELAYOUT_UNROLL)

    for pi in range(npb):
        up = up_scr[pi].astype(BF16)
        zre_scr[pl.ds(pi, ncb, stride=npb), :] = jnp.dot(
            up, qre_ref[:, pi].reshape(w, SSM_STATE_W), preferred_element_type=F32)
        zim_scr[pl.ds(pi, ncb, stride=npb), :] = jnp.dot(
            up, qim_ref[:, pi].reshape(w, SSM_STATE_W), preferred_element_type=F32)

    ar = are_ref[...]
    ai = aim_ref[...]

    def step(c, carry):
        xr, xi = carry
        row = pl.multiple_of(c * npb, npb)
        xre_scr[pl.ds(row, npb), :] = xr
        xim_scr[pl.ds(row, npb), :] = xi
        nxr = ar * xr - ai * xi + zre_scr[pl.ds(row, npb), :]
        nxi = ar * xi + ai * xr + zim_scr[pl.ds(row, npb), :]
        return nxr, nxi

    xr, xi = lax.fori_loop(0, ncb, step, (st_re[...], st_im[...]))
    st_re[...] = xr
    st_im[...] = xi

    for pi in range(npb):
        up = up_scr[pi].astype(BF16)
        xrp = xre_scr[pl.ds(pi, ncb, stride=npb), :].astype(BF16)
        xip = xim_scr[pl.ds(pi, ncb, stride=npb), :].astype(BF16)
        yp_scr[pi] = (
            jnp.dot(up, toep_scr[pi], preferred_element_type=F32)
            + lax.dot_general(xrp, ptre_ref[:, pi].reshape(w, SSM_STATE_W), nt_dims,
                              preferred_element_type=F32)
            + lax.dot_general(xip, ptim_ref[:, pi].reshape(w, SSM_STATE_W), nt_dims,
                              preferred_element_type=F32))

    def scatter(rb, carry):
        tok = pl.multiple_of(rb * rows8, rows8)
        row = pl.multiple_of(rb * 8, 8)
        for tau in range(SSM_T):
            for hf in range(2):
                yt = jnp.concatenate(
                    [yp_scr[hf * per_half + q, pl.ds(row, 8), tau * pw:(tau + 1) * pw]
                     for q in range(per_half)], axis=1)
                halves[hf][pl.ds(tok + tau, 8, stride=SSM_T), :] = yt
        return carry

    lax.fori_loop(0, ncb // 8, scatter, 0, unroll=SSM_RELAYOUT_UNROLL)
    y_ref[:, 0:LANES] = ua_scr[...]
    y_ref[:, LANES:2 * LANES] = ub_scr[...]


def _ssm_scan(rest, layer, b, s, krow, qre, qim, ptre, ptim, are, aim):
    n = rest.shape[0]
    npb = SSM_PAIR_BLOCK
    ncb = min(SSM_CHUNKS, s // SSM_T)
    tok = ncb * SSM_T
    cw = npb * SSM_PAIR_W
    nblk = s // tok
    u_col0 = R_U // cw
    w = SSM_ROW_W
    kern = functools.partial(_ssm_kernel, ncb=ncb)
    return pl.pallas_call(
        kern,
        grid=(b, SSM_PAIRS // npb, nblk),
        in_specs=[
            pl.BlockSpec((tok, cw), lambda bi, gi, ti: (bi * nblk + ti, u_col0 + gi)),
            pl.BlockSpec((None, npb, SSM_PAIR_W, w), lambda bi, gi, ti: (layer, gi, 0, 0)),
        ] + [pl.BlockSpec((None, SSM_T, npb, SSM_PAIR_W, SSM_STATE_W),
                          lambda bi, gi, ti: (layer, 0, gi, 0, 0)) for _ in range(4)] + [
            pl.BlockSpec((None, npb, SSM_STATE_W), lambda bi, gi, ti: (layer, gi, 0)),
            pl.BlockSpec((None, npb, SSM_STATE_W), lambda bi, gi, ti: (layer, gi, 0)),
        ],
        out_specs=pl.BlockSpec((tok, cw), lambda bi, gi, ti: (bi * nblk + ti, gi)),
        out_shape=jax.ShapeDtypeStruct((n, SSM_WIDTH), F32),
        scratch_shapes=[pltpu.VMEM((npb, w, w), BF16),
                        pltpu.VMEM((tok, LANES), F32), pltpu.VMEM((tok, LANES), F32),
                        pltpu.VMEM((npb, ncb, w), F32), pltpu.VMEM((npb, ncb, w), F32)]
        + [pltpu.VMEM((ncb * npb, SSM_STATE_W), F32) for _ in range(4)]
        + [pltpu.VMEM((npb, SSM_STATE_W), F32), pltpu.VMEM((npb, SSM_STATE_W), F32)],
        compiler_params=_cparams(("parallel", "parallel", "arbitrary")),
        name="s5_scan",
    )(rest, krow, qre, qim, ptre, ptim, are, aim)


def _ssm_matrices(a_re, a_im, log_dt, b_re, b_im, c_re, c_im):
    hp = lax.Precision.HIGHEST
    g, p, t, hh = SSM_GROUPS, SSM_STATE, SSM_T, SSM_GROUP
    ar, ai = a_re.astype(F32), a_im.astype(F32)
    dt = jnp.exp(log_dt.astype(F32))[:, None]
    mag = jnp.exp(ar * dt)
    ph = ai * dt
    abr, abi = mag * jnp.cos(ph), mag * jnp.sin(ph)
    nr, ni = abr - 1.0, abi
    den = ar * ar + ai * ai
    fr = (nr * ar + ni * ai) / den
    fi = (ni * ar - nr * ai) / den
    br, bi = b_re.astype(F32), b_im.astype(F32)
    bbr = fr[..., None] * br - fi[..., None] * bi
    bbi = fr[..., None] * bi + fi[..., None] * br
    cr, ci = c_re.astype(F32), c_im.astype(F32)

    jj = jnp.arange(t + 1, dtype=F32)[:, None, None]
    mag_j = jnp.exp(jj * (ar * dt)[None])
    pwr, pwi = mag_j * jnp.cos(jj * ph[None]), mag_j * jnp.sin(jj * ph[None])

    cpr = cr[None] * pwr[:t, :, None, :] - ci[None] * pwi[:t, :, None, :]
    cpi = cr[None] * pwi[:t, :, None, :] + ci[None] * pwr[:t, :, None, :]
    kj = (jnp.einsum('jgap,gph->jgah', cpr, bbr, precision=hp)
          - jnp.einsum('jgap,gph->jgah', cpi, bbi, precision=hp))
    eye2 = jnp.eye(2, dtype=F32)
    kj6 = kj.reshape(t, SSM_PAIRS, 2, hh, hh)
    krow = jnp.einsum('jpgyx,gk->pgxjky', kj6, eye2).reshape(SSM_PAIRS, SSM_PAIR_W, SSM_ROW_W)

    rr, ri = pwr[t - 1::-1][:t], pwi[t - 1::-1][:t]
    bbr_t, bbi_t = bbr.transpose(0, 2, 1), bbi.transpose(0, 2, 1)
    qr = rr[:, :, None, :] * bbr_t[None] - ri[:, :, None, :] * bbi_t[None]
    qi = rr[:, :, None, :] * bbi_t[None] + ri[:, :, None, :] * bbr_t[None]

    wr = cr[None] * pwr[1:, :, None, :] - ci[None] * pwi[1:, :, None, :]
    wi = cr[None] * pwi[1:, :, None, :] + ci[None] * pwr[1:, :, None, :]

    def pair_block_diag(m):
        m = m.reshape(t, SSM_PAIRS, 2, hh, p)
        first = (jnp.arange(2) == 0)[None, None, :, None, None]
        wide = jnp.concatenate([jnp.where(first, m, 0.0), jnp.where(first, 0.0, m)], axis=-1)
        return wide.astype(BF16).reshape(t, SSM_PAIRS, SSM_PAIR_W, 2 * p)

    are = pwr[t].reshape(SSM_PAIRS, 2 * p)
    aim = pwi[t].reshape(SSM_PAIRS, 2 * p)
    return (krow, pair_block_diag(qr), pair_block_diag(qi),
            pair_block_diag(wr), pair_block_diag(-wi), are, aim)


def _merge_kernel(ya_ref, za_ref, ys_ref, u_ref, zs_ref, ga_ref, gs_ref,
                  d_ref, wglu_ref, wao_ref, wso_ref, o_ref):
    ya = ya_ref[...] * jax.nn.silu(za_ref[...])
    y = ys_ref[...] + d_ref[...] * u_ref[...]
    y = jax.nn.gelu(y)
    y = y * jax.nn.sigmoid(jnp.dot(y.astype(BF16), wglu_ref[...], preferred_element_type=F32))
    ys = y * jax.nn.silu(zs_ref[...])
    oa = jnp.dot(ya.astype(BF16), wao_ref[...], preferred_element_type=F32)
    os_ = jnp.dot(ys.astype(BF16), wso_ref[...], preferred_element_type=F32)
    merged = jax.nn.sigmoid(ga_ref[...]) * oa + jax.nn.sigmoid(gs_ref[...]) * os_
    o_ref[...] = merged.astype(o_ref.dtype)


def _merge(ya, ys, rest, d_row, wglu, wao, wso, layer, tm):
    n = ya.shape[0]
    d = wao.shape[2]
    w1 = ATT_WIDTH
    cb = lambda off, wd: pl.BlockSpec((tm, wd), lambda i: (i, off // wd))
    return pl.pallas_call(
        _merge_kernel,
        grid=(n // tm,),
        in_specs=[cb(0, w1), cb(R_ZATT, w1), cb(0, w1), cb(R_U, w1), cb(R_ZSSM, w1),
                  cb(R_GATT, d), cb(R_GSSM, d),
                  _resident(d_row.shape, layer), _resident(wglu.shape, layer),
                  _resident(wao.shape, layer), _resident(wso.shape, layer)],
        out_specs=pl.BlockSpec((tm, d), lambda i: (i, 0)),
        out_shape=jax.ShapeDtypeStruct((n, d), BF16),
        compiler_params=_cparams(("parallel",)),
        name="branch_merge",
    )(ya, rest, ys, rest, rest, rest, rest, d_row, wglu, wao, wso)


def _out_kernel(mg_ref, gp_ref, p_ref, x_ref, wout_ref, wple_ref, g_ref, b_ref, o_ref):
    mo = jnp.dot(mg_ref[...], wout_ref[...], preferred_element_type=F32)
    ple = jax.nn.sigmoid(gp_ref[...]) * jnp.dot(
        p_ref[...].astype(BF16), wple_ref[...], preferred_element_type=F32)
    r = ALPHA * x_ref[...] + mo + ple
    mu = jnp.mean(r, axis=-1, keepdims=True)
    rc = r - mu
    var = jnp.mean(rc * rc, axis=-1, keepdims=True)
    o_ref[...] = rc * lax.rsqrt(var + LN_EPS) * g_ref[...] + b_ref[...]


def _out_ln(merged, rest, p3, x2, wout, wple, g_row, b_row, layer, tm):
    n, d = x2.shape
    return pl.pallas_call(
        _out_kernel,
        grid=(n // tm,),
        in_specs=[pl.BlockSpec((tm, d), lambda i: (i, 0)),
                  pl.BlockSpec((tm, d), lambda i: (i, R_GPLE // d)),
                  pl.BlockSpec((None, tm, p3.shape[2]), lambda i: (layer, i, 0)),
                  pl.BlockSpec((tm, d), lambda i: (i, 0)),
                  _resident(wout.shape, layer), _resident(wple.shape, layer),
                  _resident(g_row.shape, layer), _resident(b_row.shape, layer)],
        out_specs=pl.BlockSpec((tm, d), lambda i: (i, 0)),
        out_shape=jax.ShapeDtypeStruct((n, d), F32),
        compiler_params=_cparams(("parallel",)),
        name="out_ln",
    )(merged, rest, p3, x2, wout, wple, g_row, b_row)


def _key_bias_columns(slopes):
    t = ATT_TILE
    v = slopes[:, None] * jnp.arange(t, dtype=F32)[None, :]
    parts = []
    for _ in range(BIAS_PARTS):
        top = lax.bitcast_convert_type(
            lax.bitcast_convert_type(v, jnp.uint32) & jnp.uint32(0xFFFF0000), F32)
        parts.append(top)
        v = v - top
    cols = jnp.stack(parts, axis=-1)
    pad = jnp.zeros((HEADS, t, ATT_K - 2 * DK - BIAS_PARTS), F32)
    return jnp.concatenate([cols, pad], axis=-1).astype(BF16)


def _diag_tile_terms(slopes):
    t = ATT_TILE
    ik = jnp.arange(t, dtype=jnp.int32)[:, None]
    iq = jnp.arange(t, dtype=jnp.int32)[None, :]
    ahead = jnp.maximum(ik - iq, 0).astype(F32)
    allowed = (ik // CHUNK) <= (iq // CHUNK)
    return jnp.where(allowed[None], -2.0 * slopes[:, None, None] * ahead[None], NEG_INF)


def kernel(x, p, w_in, w_att_out, w_ssm_out, w_out, w_ple, lambda_q1, lambda_k1, lambda_q2,
           lambda_k2, subln_w, ssm_a_re, ssm_a_im, ssm_log_dt, ssm_b_re, ssm_b_im, ssm_c_re,
           ssm_c_im, ssm_d, ssm_w_glu, ln_g, ln_b):
    b, s, d = x.shape
    n = b * s
    x2 = x.reshape(n, d)
    p3 = p.reshape(DEPTH, n, PLE_DIM)

    lambda_inits = [0.8 - 0.6 * math.exp(-0.3 * i) for i in range(DEPTH)]
    lam = (jnp.exp(jnp.sum(lambda_q1.astype(F32) * lambda_k1.astype(F32), axis=-1))
           - jnp.exp(jnp.sum(lambda_q2.astype(F32) * lambda_k2.astype(F32), axis=-1))
           + jnp.asarray(lambda_inits, F32))
    slopes = LOG2E * 2.0 ** (-8.0 * (jnp.arange(HEADS, dtype=F32) + 1.0) / HEADS)
    kbias = _key_bias_columns(slopes)
    diag = _diag_tile_terms(slopes)
    scale = LOG2E / math.sqrt(DK)
    wqvT = _wqv_transposed(w_in.astype(F32), scale)
    w_in_b = w_in.astype(BF16)
    w_glu_b, w_ao_b, w_so_b = (ssm_w_glu.astype(BF16), w_att_out.astype(BF16),
                               w_ssm_out.astype(BF16))
    w_out_b, w_ple_b = w_out.astype(BF16), w_ple.astype(BF16)
    subln_col = subln_w.astype(F32).reshape(DEPTH, DV, 1)
    d_row = ssm_d.astype(F32).reshape(DEPTH, 1, SSM_WIDTH)
    g_row = ln_g.astype(F32).reshape(DEPTH, 1, d)
    b_row = ln_b.astype(F32).reshape(DEPTH, 1, d)
    mats = jax.vmap(_ssm_matrices)(ssm_a_re, ssm_a_im, ssm_log_dt, ssm_b_re, ssm_b_im,
                                   ssm_c_re, ssm_c_im)

    for i in range(DEPTH):
        qT, ka, vT = _qkv_proj(x2, wqvT, w_in_b, kbias, i, b, s)
        rest = _rest_proj(x2, w_in_b, i, 1024, 1024)
        y_att = _attention(slopes, lam, qT, ka, vT, diag, subln_col, i, lambda_inits[i])
        ys = _ssm_scan(rest, i, b, s, *mats)
        merged = _merge(y_att.reshape(n, ATT_WIDTH), ys, rest, d_row, w_glu_b, w_ao_b, w_so_b,
                        i, 256)
        x2 = _out_ln(merged, rest, p3, x2, w_out_b, w_ple_b, g_row, b_row, i, 256)
    return x2.reshape(b, s, d)
```

```python
import functools
import math

import jax
import jax.numpy as jnp
from jax import lax
from jax.experimental import pallas as pl
from jax.experimental.pallas import tpu as pltpu

F32 = jnp.float32
BF16 = jnp.bfloat16
LANES = 128

DEPTH = 2
PLE_DIM = 256
CHUNK = 64
HEADS = 8
DK = 64
DV = 128
ATT_WIDTH = HEADS * DV
SSM_WIDTH = 1024
SSM_GROUP = 16
SSM_GROUPS = SSM_WIDTH // SSM_GROUP
SSM_STATE = 64
ALPHA = (2.0 * DEPTH) ** 0.25
LN_EPS = 1e-5
RMS_EPS = 1e-5
NEG_INF = -1e30

COL_Q, COL_K, COL_V, COL_REST = 0, 1024, 2048, 3072
COL_GATES = 6144
R_GATT, R_GSSM, R_GPLE, R_ZATT, R_U, R_ZSSM = 0, 2048, 4096, 6144, 7168, 8192
REST_COLS = 9216

ATT_TILE = 512
ATT_K = 256
ATT_GROUPS = (8, 4, 2)
BIAS_PARTS = 3
DV_AUG = DV + 16
LOG2E = 1.4426950408889634
QKV_ROWS = 256
WQV_BLOCK = 512
SSM_T = 16
SSM_PAIRS = SSM_GROUPS // 2
SSM_PAIR_W = 2 * SSM_GROUP
SSM_ROW_W = SSM_T * SSM_PAIR_W
SSM_STATE_W = 2 * SSM_STATE
SSM_PAIR_BLOCK = 8
SSM_CHUNKS = 128
SSM_RELAYOUT_UNROLL = 4
VMEM_LIMIT = 48 * 1024 * 1024


def _cparams(sem):
    return pltpu.CompilerParams(dimension_semantics=sem, vmem_limit_bytes=VMEM_LIMIT)


def _resident(stacked_shape, layer, col_block=0):
    idx = (layer, 0, col_block)
    return pl.BlockSpec((None,) + tuple(stacked_shape[1:]), lambda *_: idx,
                        pipeline_mode=pl.Buffered(1))


def _wqv_kernel(w_ref, o_ref, *, q_blocks, q_scale):
    scale = jnp.where(pl.program_id(1) < q_blocks, q_scale, 1.0)
    o_ref[...] = (w_ref[...] * scale).T.astype(o_ref.dtype)


def _wqv_transposed(w_in, q_scale):
    depth, d, _ = w_in.shape
    blk = WQV_BLOCK
    q_blocks = (COL_K - COL_Q) // blk
    v_first = COL_V // blk
    out_rows = (COL_K - COL_Q) + (COL_REST - COL_V)
    kern = functools.partial(_wqv_kernel, q_blocks=q_blocks, q_scale=q_scale)
    return pl.pallas_call(
        kern,
        grid=(depth, out_rows // blk, d // blk),
        in_specs=[pl.BlockSpec(
            (None, blk, blk),
            lambda l, c, r: (l, r, jnp.where(c < q_blocks, c, c - q_blocks + v_first)))],
        out_specs=pl.BlockSpec((None, blk, blk), lambda l, c, r: (l, c, r)),
        out_shape=jax.ShapeDtypeStruct((depth, out_rows, d), BF16),
        compiler_params=_cparams(("parallel", "parallel", "parallel")),
        name="wqv_transpose",
    )(w_in)


def _qkv_kernel(x_ref, wqv_ref, wk_ref, kb_ref, qT_ref, ka_ref, vT_ref):
    xb = x_ref[...].astype(BF16)
    tm = xb.shape[0]
    nt = (((1,), (1,)), ((), ()))
    qv = lax.dot_general(wqv_ref[...], xb, nt, preferred_element_type=F32)
    k = jnp.dot(xb, wk_ref[...], preferred_element_type=F32)

    row = lax.broadcasted_iota(jnp.int32, (ATT_K - 2 * DK, tm), 0)
    ones_rows = jnp.where(row < BIAS_PARTS, 1.0, 0.0).astype(BF16)
    zeros = jnp.zeros((DK, tm), BF16)
    sum_rows = jnp.ones((DV_AUG - DV, tm), BF16)
    for h in range(HEADS):
        q1 = qv[(2 * h) * DK:(2 * h + 1) * DK].astype(BF16)
        q2 = qv[(2 * h + 1) * DK:(2 * h + 2) * DK].astype(BF16)
        qT_ref[0, h, 0] = jnp.concatenate([q1, zeros, ones_rows], axis=0)
        qT_ref[0, h, 1] = jnp.concatenate([zeros, q2, ones_rows], axis=0)
        ka_ref[0, h, :, 0:2 * DK] = k[:, h * 2 * DK:(h + 1) * 2 * DK].astype(BF16)
        ka_ref[0, h, :, 2 * DK:ATT_K] = kb_ref[h]
        v0 = HEADS * 2 * DK + h * DV
        vT_ref[0, h, 0] = jnp.concatenate([qv[v0:v0 + DV].astype(BF16), sum_rows], axis=0)


def _qkv_proj(x2, wqvT, w_in, kbias, layer, b, s):
    n, d = x2.shape
    tm = QKV_ROWS
    t = ATT_TILE
    per_b = s // tm
    sub = t // tm
    k_width = COL_V - COL_K
    return pl.pallas_call(
        _qkv_kernel,
        grid=(n // tm,),
        in_specs=[pl.BlockSpec((tm, d), lambda i: (i, 0)),
                  _resident(wqvT.shape, layer),
                  _resident((DEPTH, d, k_width), layer, COL_K // k_width),
                  pl.BlockSpec((HEADS, tm, ATT_K - 2 * DK), lambda i: (0, i % sub, 0))],
        out_specs=[
            pl.BlockSpec((1, HEADS, 2, ATT_K, tm), lambda i: (i // per_b, 0, 0, 0, i % per_b)),
            pl.BlockSpec((1, HEADS, tm, ATT_K), lambda i: (i // per_b, 0, i % per_b, 0)),
            pl.BlockSpec((1, HEADS, 1, DV_AUG, tm),
                         lambda i: (i // per_b, 0, (i % per_b) // sub, 0, i % sub)),
        ],
        out_shape=[jax.ShapeDtypeStruct((b, HEADS, 2, ATT_K, s), BF16),
                   jax.ShapeDtypeStruct((b, HEADS, s, ATT_K), BF16),
                   jax.ShapeDtypeStruct((b, HEADS, s // t, DV_AUG, t), BF16)],
        compiler_params=_cparams(("parallel",)),
        name="qkv_proj",
    )(x2, wqvT, w_in, kbias)


def _proj_kernel(x_ref, w_ref, o_ref):
    o_ref[...] = jnp.dot(x_ref[...].astype(BF16), w_ref[...],
                         preferred_element_type=F32).astype(o_ref.dtype)


def _rest_proj(x2, w_in, layer, tm, tn):
    n, d = x2.shape
    gate_blocks = (w_in.shape[2] - COL_GATES) // tn
    first_gate = COL_GATES // tn
    first_rest = COL_REST // tn

    def w_block(i, j):
        return (layer, 0, jnp.where(j < gate_blocks, j + first_gate, j - gate_blocks + first_rest))

    return pl.pallas_call(
        _proj_kernel,
        grid=(n // tm, REST_COLS // tn),
        in_specs=[pl.BlockSpec((tm, d), lambda i, j: (i, 0)),
                  pl.BlockSpec((None, d, tn), w_block)],
        out_specs=pl.BlockSpec((tm, tn), lambda i, j: (i, j)),
        out_shape=jax.ShapeDtypeStruct((n, REST_COLS), F32),
        compiler_params=_cparams(("parallel", "arbitrary")),
        name="in_proj",
    )(x2, w_in)


def _attn_kernel(slope_ref, lam_ref, qT_ref, qTn_ref, ka_ref, vT_ref, dg_ref, w_ref, o_ref,
                 s_scr, cm_scr, m_scr, acc_scr, *, tile, layer, out_scale):
    h = pl.program_id(1)
    qi = pl.program_id(2)
    slope = slope_ref[h]
    lam = lam_ref[layer]

    m_scr[...] = jnp.full(m_scr.shape, NEG_INF, F32)
    acc_scr[...] = jnp.zeros(acc_scr.shape, F32)

    def scores(j, slot, q_ref=qT_ref):
        kt = ka_ref[0, 0, pl.ds(pl.multiple_of(j * tile, tile), tile), :]
        for mp in range(2):
            u = jnp.dot(kt, q_ref[0, 0, mp], preferred_element_type=F32)
            s_scr[slot, mp] = u
            cm_scr[slot, mp] = jnp.max(u, axis=0, keepdims=True)

    def update(mp, u, cm, shift, j):
        m_old = m_scr[mp]
        m_new = jnp.maximum(m_old, cm + shift)
        a = jnp.exp2(m_old - m_new)
        p = jnp.exp2(u - (m_new - shift))
        pv = jnp.dot(vT_ref[0, 0, j], p.astype(BF16), preferred_element_type=F32)
        acc_scr[mp] = a * acc_scr[mp] + pv
        m_scr[mp] = m_new

    @pl.when(qi == 0)
    def _():
        scores(0, 0)

    def past_tile(j, src, dst):
        shift = slope * ((j - qi) * tile).astype(F32)
        us = [s_scr[src, mp] for mp in range(2)]
        cms = [cm_scr[src, mp] for mp in range(2)]
        scores(j + 1, dst)
        for mp in range(2):
            update(mp, us[mp], cms[mp], shift, j)

    def past_single(j, carry):
        past_tile(j, 0, 0)
        return carry

    done = 0
    for size in ATT_GROUPS:
        def past_group(jj, carry, size=size, base=done):
            for r in range(size):
                past_tile(base + size * jj + r, r % 2, (r + 1) % 2)
            return carry

        trips = (qi - done) // size
        lax.fori_loop(0, trips, past_group, 0)
        done = done + trips * size
    lax.fori_loop(done, qi, past_single, 0)

    us = [s_scr[0, mp] + dg_ref[0] for mp in range(2)]
    scores(0, 0, qTn_ref)
    for mp in range(2):
        update(mp, us[mp], jnp.max(us[mp], axis=0, keepdims=True), jnp.float32(0.0), qi)

    o = (acc_scr[0, 0:DV] / acc_scr[0, DV:DV + 1]
         - lam * (acc_scr[1, 0:DV] / acc_scr[1, DV:DV + 1]))
    ms = jnp.mean(o * o, axis=0, keepdims=True)
    o = o * lax.rsqrt(ms + RMS_EPS)
    o = o * w_ref[...] * out_scale
    o_ref[0] = o.T


def _attention(slopes, lam, qT, ka, vT, diag, subln_col, layer, lambda_init):
    b, h, _, _, s = qT.shape
    t = ATT_TILE
    nt = s // t
    kern = functools.partial(_attn_kernel, tile=t, layer=layer, out_scale=1.0 - lambda_init)
    return pl.pallas_call(
        kern,
        grid=(b, h, nt),
        in_specs=[
            pl.BlockSpec(memory_space=pltpu.SMEM),
            pl.BlockSpec(memory_space=pltpu.SMEM),
            pl.BlockSpec((1, 1, 2, ATT_K, t), lambda bi, hi, qi: (bi, hi, 0, 0, qi)),
            pl.BlockSpec((1, 1, 2, ATT_K, t),
                         lambda bi, hi, qi: (bi, hi, 0, 0, jnp.minimum(qi + 1, nt - 1))),
            pl.BlockSpec((1, 1, s, ATT_K), lambda bi, hi, qi: (bi, hi, 0, 0)),
            pl.BlockSpec((1, 1, nt, DV_AUG, t), lambda bi, hi, qi: (bi, hi, 0, 0, 0)),
            pl.BlockSpec((1, t, t), lambda bi, hi, qi: (hi, 0, 0)),
            pl.BlockSpec((None, DV, 1), lambda bi, hi, qi: (layer, 0, 0)),
        ],
        out_specs=pl.BlockSpec((1, t, DV), lambda bi, hi, qi: (bi, qi, hi)),
        out_shape=jax.ShapeDtypeStruct((b, s, ATT_WIDTH), F32),
        scratch_shapes=[pltpu.VMEM((2, 2, t, t), F32),
                        pltpu.VMEM((2, 2, 1, t), F32),
                        pltpu.VMEM((2, 1, t), F32),
                        pltpu.VMEM((2, DV_AUG, t), F32)],
        compiler_params=_cparams(("arbitrary", "arbitrary", "arbitrary")),
        name="diff_attn",
    )(slopes, lam, qT, qT, ka, vT, diag, subln_col)


def _ssm_kernel(u_ref, krow_ref, qre_ref, qim_ref, ptre_ref, ptim_ref, are_ref, aim_ref,
                y_ref, toep_scr, ua_scr, ub_scr, up_scr, yp_scr, zre_scr, zim_scr, xre_scr,
                xim_scr, st_re, st_im, *, ncb):
    npb = SSM_PAIR_BLOCK
    pw = SSM_PAIR_W
    w = SSM_ROW_W
    rows8 = 8 * SSM_T
    halves = (ua_scr, ub_scr)
    per_half = LANES // pw
    nt_dims = (((1,), (1,)), ((), ()))

    @pl.when(pl.program_id(2) == 0)
    def _():
        st_re[...] = jnp.zeros(st_re.shape, F32)
        st_im[...] = jnp.zeros(st_im.shape, F32)
        lane = lax.broadcasted_iota(jnp.int32, (pw, w), 1)
        for pi in range(npb):
            kr = krow_ref[pi]
            for tau in range(SSM_T):
                blk = kr if tau == 0 else jnp.where(
                    lane >= tau * pw, pltpu.roll(kr, tau * pw, axis=1), 0.0)
                toep_scr[pi, tau * pw:(tau + 1) * pw, :] = blk.astype(BF16)

    ua_scr[...] = u_ref[:, 0:LANES]
    ub_scr[...] = u_ref[:, LANES:2 * LANES]

    def gather(rb, carry):
        tok = pl.multiple_of(rb * rows8, rows8)
        row = pl.multiple_of(rb * 8, 8)
        for tau in range(SSM_T):
            for hf in range(2):
                ut = halves[hf][pl.ds(tok + tau, 8, stride=SSM_T), :]
                for q in range(per_half):
                    up_scr[hf * per_half + q, pl.ds(row, 8), tau * pw:(tau + 1) * pw] = (
                        ut[:, q * pw:(q + 1) * pw])
        return carry

    lax.fori_loop(0, ncb // 8, gather, 0, unroll=SSM_RELAYOUT_UNROLL)

    for pi in range(npb):
        up = up_scr[pi].astype(BF16)
        zre_scr[pl.ds(pi, ncb, stride=npb), :] = jnp.dot(
            up, qre_ref[:, pi].reshape(w, SSM_STATE_W), preferred_element_type=F32)
        zim_scr[pl.ds(pi, ncb, stride=npb), :] = jnp.dot(
            up, qim_ref[:, pi].reshape(w, SSM_STATE_W), preferred_element_type=F32)

    ar = are_ref[...]
    ai = aim_ref[...]

    def step(c, carry):
        xr, xi = carry
        row = pl.multiple_of(c * npb, npb)
        xre_scr[pl.ds(row, npb), :] = xr
        xim_scr[pl.ds(row, npb), :] = xi
        nxr = ar * xr - ai * xi + zre_scr[pl.ds(row, npb), :]
        nxi = ar * xi + ai * xr + zim_scr[pl.ds(row, npb), :]
        return nxr, nxi

    xr, xi = lax.fori_loop(0, ncb, step, (st_re[...], st_im[...]))
    st_re[...] = xr
    st_im[...] = xi

    for pi in range(npb):
        up = up_scr[pi].astype(BF16)
        xrp = xre_scr[pl.ds(pi, ncb, stride=npb), :].astype(BF16)
        xip = xim_scr[pl.ds(pi, ncb, stride=npb), :].astype(BF16)
        yp_scr[pi] = (
            jnp.dot(up, toep_scr[pi], preferred_element_type=F32)
            + lax.dot_general(xrp, ptre_ref[:, pi].reshape(w, SSM_STATE_W), nt_dims,
                              preferred_element_type=F32)
            + lax.dot_general(xip, ptim_ref[:, pi].reshape(w, SSM_STATE_W), nt_dims,
                              preferred_element_type=F32))

    def scatter(rb, carry):
        tok = pl.multiple_of(rb * rows8, rows8)
        row = pl.multiple_of(rb * 8, 8)
        for tau in range(SSM_T):
            for hf in range(2):
                yt = jnp.concatenate(
                    [yp_scr[hf * per_half + q, pl.ds(row, 8), tau * pw:(tau + 1) * pw]
                     for q in range(per_half)], axis=1)
                halves[hf][pl.ds(tok + tau, 8, stride=SSM_T), :] = yt
        return carry

    lax.fori_loop(0, ncb // 8, scatter, 0, unroll=SSM_RELAYOUT_UNROLL)
    y_ref[:, 0:LANES] = ua_scr[...]
    y_ref[:, LANES:2 * LANES] = ub_scr[...]


def _ssm_scan(rest, layer, b, s, krow, qre, qim, ptre, ptim, are, aim):
    n = rest.shape[0]
    npb = SSM_PAIR_BLOCK
    ncb = min(SSM_CHUNKS, s // SSM_T)
    tok = ncb * SSM_T
    cw = npb * SSM_PAIR_W
    nblk = s // tok
    u_col0 = R_U // cw
    w = SSM_ROW_W
    kern = functools.partial(_ssm_kernel, ncb=ncb)
    return pl.pallas_call(
        kern,
        grid=(b, SSM_PAIRS // npb, nblk),
        in_specs=[
            pl.BlockSpec((tok, cw), lambda bi, gi, ti: (bi * nblk + ti, u_col0 + gi)),
            pl.BlockSpec((None, npb, SSM_PAIR_W, w), lambda bi, gi, ti: (layer, gi, 0, 0)),
        ] + [pl.BlockSpec((None, SSM_T, npb, SSM_PAIR_W, SSM_STATE_W),
                          lambda bi, gi, ti: (layer, 0, gi, 0, 0)) for _ in range(4)] + [
            pl.BlockSpec((None, npb, SSM_STATE_W), lambda bi, gi, ti: (layer, gi, 0)),
            pl.BlockSpec((None, npb, SSM_STATE_W), lambda bi, gi, ti: (layer, gi, 0)),
        ],
        out_specs=pl.BlockSpec((tok, cw), lambda bi, gi, ti: (bi * nblk + ti, gi)),
        out_shape=jax.ShapeDtypeStruct((n, SSM_WIDTH), F32),
        scratch_shapes=[pltpu.VMEM((npb, w, w), BF16),
                        pltpu.VMEM((tok, LANES), F32), pltpu.VMEM((tok, LANES), F32),
                        pltpu.VMEM((npb, ncb, w), F32), pltpu.VMEM((npb, ncb, w), F32)]
        + [pltpu.VMEM((ncb * npb, SSM_STATE_W), F32) for _ in range(4)]
        + [pltpu.VMEM((npb, SSM_STATE_W), F32), pltpu.VMEM((npb, SSM_STATE_W), F32)],
        compiler_params=_cparams(("parallel", "parallel", "arbitrary")),
        name="s5_scan",
    )(rest, krow, qre, qim, ptre, ptim, are, aim)


def _ssm_matrices(a_re, a_im, log_dt, b_re, b_im, c_re, c_im):
    hp = lax.Precision.HIGHEST
    g, p, t, hh = SSM_GROUPS, SSM_STATE, SSM_T, SSM_GROUP
    ar, ai = a_re.astype(F32), a_im.astype(F32)
    dt = jnp.exp(log_dt.astype(F32))[:, None]
    mag = jnp.exp(ar * dt)
    ph = ai * dt
    abr, abi = mag * jnp.cos(ph), mag * jnp.sin(ph)
    nr, ni = abr - 1.0, abi
    den = ar * ar + ai * ai
    fr = (nr * ar + ni * ai) / den
    fi = (ni * ar - nr * ai) / den
    br, bi = b_re.astype(F32), b_im.astype(F32)
    bbr = fr[..., None] * br - fi[..., None] * bi
    bbi = fr[..., None] * bi + fi[..., None] * br
    cr, ci = c_re.astype(F32), c_im.astype(F32)

    jj = jnp.arange(t + 1, dtype=F32)[:, None, None]
    mag_j = jnp.exp(jj * (ar * dt)[None])
    pwr, pwi = mag_j * jnp.cos(jj * ph[None]), mag_j * jnp.sin(jj * ph[None])

    cpr = cr[None] * pwr[:t, :, None, :] - ci[None] * pwi[:t, :, None, :]
    cpi = cr[None] * pwi[:t, :, None, :] + ci[None] * pwr[:t, :, None, :]
    kj = jnp.einsum('jgap,gph->jgah', jnp.concatenate([cpr, -cpi], axis=-1),
                    jnp.concatenate([bbr, bbi], axis=1), precision=hp)
    eye2 = jnp.eye(2, dtype=F32)
    kj6 = kj.reshape(t, SSM_PAIRS, 2, hh, hh)
    krow = jnp.einsum('jpgyx,gk->pgxjky', kj6, eye2).reshape(SSM_PAIRS, SSM_PAIR_W, SSM_ROW_W)

    rr, ri = pwr[t - 1::-1][:t], pwi[t - 1::-1][:t]
    bbr_t, bbi_t = bbr.transpose(0, 2, 1), bbi.transpose(0, 2, 1)
    qr = rr[:, :, None, :] * bbr_t[None] - ri[:, :, None, :] * bbi_t[None]
    qi = rr[:, :, None, :] * bbi_t[None] + ri[:, :, None, :] * bbr_t[None]

    wr = cr[None] * pwr[1:, :, None, :] - ci[None] * pwi[1:, :, None, :]
    wi = cr[None] * pwi[1:, :, None, :] + ci[None] * pwr[1:, :, None, :]

    def pair_block_diag(m):
        m = m.reshape(t, SSM_PAIRS, 2, hh, p)
        first = (jnp.arange(2) == 0)[None, None, :, None, None]
        wide = jnp.concatenate([jnp.where(first, m, 0.0), jnp.where(first, 0.0, m)], axis=-1)
        return wide.astype(BF16).reshape(t, SSM_PAIRS, SSM_PAIR_W, 2 * p)

    are = pwr[t].reshape(SSM_PAIRS, 2 * p)
    aim = pwi[t].reshape(SSM_PAIRS, 2 * p)
    return (krow, pair_block_diag(qr), pair_block_diag(qi),
            pair_block_diag(wr), pair_block_diag(-wi), are, aim)


def _merge_kernel(ya_ref, za_ref, ys_ref, u_ref, zs_ref, ga_ref, gs_ref,
                  d_ref, wglu_ref, wao_ref, wso_ref, o_ref):
    ya = ya_ref[...] * jax.nn.silu(za_ref[...])
    y = ys_ref[...] + d_ref[...] * u_ref[...]
    y = jax.nn.gelu(y)
    y = y * jax.nn.sigmoid(jnp.dot(y.astype(BF16), wglu_ref[...], preferred_element_type=F32))
    ys = y * jax.nn.silu(zs_ref[...])
    oa = jnp.dot(ya.astype(BF16), wao_ref[...], preferred_element_type=F32)
    os_ = jnp.dot(ys.astype(BF16), wso_ref[...], preferred_element_type=F32)
    merged = jax.nn.sigmoid(ga_ref[...]) * oa + jax.nn.sigmoid(gs_ref[...]) * os_
    o_ref[...] = merged.astype(o_ref.dtype)


def _merge(ya, ys, rest, d_row, wglu, wao, wso, layer, tm):
    n = ya.shape[0]
    d = wao.shape[2]
    w1 = ATT_WIDTH
    cb = lambda off, wd: pl.BlockSpec((tm, wd), lambda i: (i, off // wd))
    return pl.pallas_call(
        _merge_kernel,
        grid=(n // tm,),
        in_specs=[cb(0, w1), cb(R_ZATT, w1), cb(0, w1), cb(R_U, w1), cb(R_ZSSM, w1),
                  cb(R_GATT, d), cb(R_GSSM, d),
                  _resident(d_row.shape, layer), _resident(wglu.shape, layer),
                  _resident(wao.shape, layer), _resident(wso.shape, layer)],
        out_specs=pl.BlockSpec((tm, d), lambda i: (i, 0)),
        out_shape=jax.ShapeDtypeStruct((n, d), BF16),
        compiler_params=_cparams(("parallel",)),
        name="branch_merge",
    )(ya, rest, ys, rest, rest, rest, rest, d_row, wglu, wao, wso)


def _out_kernel(mg_ref, gp_ref, p_ref, x_ref, wout_ref, wple_ref, g_ref, b_ref, o_ref):
    mo = jnp.dot(mg_ref[...], wout_ref[...], preferred_element_type=F32)
    ple = jax.nn.sigmoid(gp_ref[...]) * jnp.dot(
        p_ref[...].astype(BF16), wple_ref[...], preferred_element_type=F32)
    r = ALPHA * x_ref[...] + mo + ple
    mu = jnp.mean(r, axis=-1, keepdims=True)
    rc = r - mu
    var = jnp.mean(rc * rc, axis=-1, keepdims=True)
    o_ref[...] = rc * lax.rsqrt(var + LN_EPS) * g_ref[...] + b_ref[...]


def _out_ln(merged, rest, p3, x2, wout, wple, g_row, b_row, layer, tm):
    n, d = x2.shape
    return pl.pallas_call(
        _out_kernel,
        grid=(n // tm,),
        in_specs=[pl.BlockSpec((tm, d), lambda i: (i, 0)),
                  pl.BlockSpec((tm, d), lambda i: (i, R_GPLE // d)),
                  pl.BlockSpec((None, tm, p3.shape[2]), lambda i: (layer, i, 0)),
                  pl.BlockSpec((tm, d), lambda i: (i, 0)),
                  _resident(wout.shape, layer), _resident(wple.shape, layer),
                  _resident(g_row.shape, layer), _resident(b_row.shape, layer)],
        out_specs=pl.BlockSpec((tm, d), lambda i: (i, 0)),
        out_shape=jax.ShapeDtypeStruct((n, d), F32),
        compiler_params=_cparams(("parallel",)),
        name="out_ln",
    )(merged, rest, p3, x2, wout, wple, g_row, b_row)


def _key_bias_columns(slopes):
    t = ATT_TILE
    v = slopes[:, None] * jnp.arange(t, dtype=F32)[None, :]
    parts = []
    for _ in range(BIAS_PARTS):
        top = lax.bitcast_convert_type(
            lax.bitcast_convert_type(v, jnp.uint32) & jnp.uint32(0xFFFF0000), F32)
        parts.append(top)
        v = v - top
    cols = jnp.stack(parts, axis=-1)
    pad = jnp.zeros((HEADS, t, ATT_K - 2 * DK - BIAS_PARTS), F32)
    return jnp.concatenate([cols, pad], axis=-1).astype(BF16)


def _diag_tile_terms(slopes):
    t = ATT_TILE
    ik = jnp.arange(t, dtype=jnp.int32)[:, None]
    iq = jnp.arange(t, dtype=jnp.int32)[None, :]
    ahead = jnp.maximum(ik - iq, 0).astype(F32)
    allowed = (ik // CHUNK) <= (iq // CHUNK)
    return jnp.where(allowed[None], -2.0 * slopes[:, None, None] * ahead[None], NEG_INF)


def kernel(x, p, w_in, w_att_out, w_ssm_out, w_out, w_ple, lambda_q1, lambda_k1, lambda_q2,
           lambda_k2, subln_w, ssm_a_re, ssm_a_im, ssm_log_dt, ssm_b_re, ssm_b_im, ssm_c_re,
           ssm_c_im, ssm_d, ssm_w_glu, ln_g, ln_b):
    b, s, d = x.shape
    n = b * s
    x2 = x.reshape(n, d)
    p3 = p.reshape(DEPTH, n, PLE_DIM)

    lambda_inits = [0.8 - 0.6 * math.exp(-0.3 * i) for i in range(DEPTH)]
    lam = (jnp.exp(jnp.sum(lambda_q1.astype(F32) * lambda_k1.astype(F32), axis=-1))
           - jnp.exp(jnp.sum(lambda_q2.astype(F32) * lambda_k2.astype(F32), axis=-1))
           + jnp.asarray(lambda_inits, F32))
    slopes = LOG2E * 2.0 ** (-8.0 * (jnp.arange(HEADS, dtype=F32) + 1.0) / HEADS)
    kbias = _key_bias_columns(slopes)
    diag = _diag_tile_terms(slopes)
    scale = LOG2E / math.sqrt(DK)
    wqvT = _wqv_transposed(w_in.astype(F32), scale)
    w_in_b = w_in.astype(BF16)
    w_glu_b, w_ao_b, w_so_b = (ssm_w_glu.astype(BF16), w_att_out.astype(BF16),
                               w_ssm_out.astype(BF16))
    w_out_b, w_ple_b = w_out.astype(BF16), w_ple.astype(BF16)
    subln_col = subln_w.astype(F32).reshape(DEPTH, DV, 1)
    d_row = ssm_d.astype(F32).reshape(DEPTH, 1, SSM_WIDTH)
    g_row = ln_g.astype(F32).reshape(DEPTH, 1, d)
    b_row = ln_b.astype(F32).reshape(DEPTH, 1, d)
    mats = jax.vmap(_ssm_matrices)(ssm_a_re, ssm_a_im, ssm_log_dt, ssm_b_re, ssm_b_im,
                                   ssm_c_re, ssm_c_im)

    for i in range(DEPTH):
        qT, ka, vT = _qkv_proj(x2, wqvT, w_in_b, kbias, i, b, s)
        rest = _rest_proj(x2, w_in_b, i, 1024, 1024)
        y_att = _attention(slopes, lam, qT, ka, vT, diag, subln_col, i, lambda_inits[i])
        ys = _ssm_scan(rest, i, b, s, *mats)
        merged = _merge(y_att.reshape(n, ATT_WIDTH), ys, rest, d_row, w_glu_b, w_ao_b, w_so_b,
                        i, 256)
        x2 = _out_ln(merged, rest, p3, x2, w_out_b, w_ple_b, g_row, b_row, i, 256)
    return x2.reshape(b, s, d)
```

```python
import functools
import math

import jax
import jax.numpy as jnp
from jax import lax
from jax.experimental import pallas as pl
from jax.experimental.pallas import tpu as pltpu

F32 = jnp.float32
BF16 = jnp.bfloat16
LANES = 128

DEPTH = 2
PLE_DIM = 256
CHUNK = 64
HEADS = 8
DK = 64
DV = 128
ATT_WIDTH = HEADS * DV
SSM_WIDTH = 1024
SSM_GROUP = 16
SSM_GROUPS = SSM_WIDTH // SSM_GROUP
SSM_STATE = 64
ALPHA = (2.0 * DEPTH) ** 0.25
LN_EPS = 1e-5
RMS_EPS = 1e-5
NEG_INF = -1e30

COL_Q, COL_K, COL_V, COL_REST = 0, 1024, 2048, 3072
COL_GATES = 6144
R_GATT, R_GSSM, R_GPLE, R_ZATT, R_U, R_ZSSM = 0, 2048, 4096, 6144, 7168, 8192
REST_COLS = 9216

ATT_TILE = 512
ATT_K = 256
ATT_GROUPS = (8, 4, 2)
BIAS_PARTS = 3
DV_AUG = DV + 16
LOG2E = 1.4426950408889634
QKV_ROWS = 256
WQV_BLOCK = 512
SSM_T = 16
SSM_PAIRS = SSM_GROUPS // 2
SSM_PAIR_W = 2 * SSM_GROUP
SSM_ROW_W = SSM_T * SSM_PAIR_W
SSM_STATE_W = 2 * SSM_STATE
SSM_PAIR_BLOCK = 8
SSM_CHUNKS = 128
SSM_RELAYOUT_UNROLL = 4
VMEM_LIMIT = 48 * 1024 * 1024


def _cparams(sem):
    return pltpu.CompilerParams(dimension_semantics=sem, vmem_limit_bytes=VMEM_LIMIT)


def _resident(stacked_shape, layer, col_block=0):
    idx = (layer, 0, col_block)
    return pl.BlockSpec((None,) + tuple(stacked_shape[1:]), lambda *_: idx,
                        pipeline_mode=pl.Buffered(1))


def _wqv_kernel(w_ref, o_ref, *, q_blocks, q_scale):
    scale = jnp.where(pl.program_id(1) < q_blocks, q_scale, 1.0)
    o_ref[...] = (w_ref[...] * scale).T.astype(o_ref.dtype)


def _wqv_transposed(w_in, q_scale):
    depth, d, _ = w_in.shape
    blk = WQV_BLOCK
    q_blocks = (COL_K - COL_Q) // blk
    v_first = COL_V // blk
    out_rows = (COL_K - COL_Q) + (COL_REST - COL_V)
    kern = functools.partial(_wqv_kernel, q_blocks=q_blocks, q_scale=q_scale)
    return pl.pallas_call(
        kern,
        grid=(depth, out_rows // blk, d // blk),
        in_specs=[pl.BlockSpec(
            (None, blk, blk),
            lambda l, c, r: (l, r, jnp.where(c < q_blocks, c, c - q_blocks + v_first)))],
        out_specs=pl.BlockSpec((None, blk, blk), lambda l, c, r: (l, c, r)),
        out_shape=jax.ShapeDtypeStruct((depth, out_rows, d), BF16),
        compiler_params=_cparams(("parallel", "parallel", "parallel")),
        name="wqv_transpose",
    )(w_in)


def _qkv_kernel(x_ref, wqv_ref, wk_ref, kb_ref, qT_ref, ka_ref, vT_ref):
    xb = x_ref[...].astype(BF16)
    tm = xb.shape[0]
    nt = (((1,), (1,)), ((), ()))
    qv = lax.dot_general(wqv_ref[...], xb, nt, preferred_element_type=F32)
    k = jnp.dot(xb, wk_ref[...], preferred_element_type=F32)

    row = lax.broadcasted_iota(jnp.int32, (ATT_K - 2 * DK, tm), 0)
    ones_rows = jnp.where(row < BIAS_PARTS, 1.0, 0.0).astype(BF16)
    zeros = jnp.zeros((DK, tm), BF16)
    sum_rows = jnp.ones((DV_AUG - DV, tm), BF16)
    for h in range(HEADS):
        q1 = qv[(2 * h) * DK:(2 * h + 1) * DK].astype(BF16)
        q2 = qv[(2 * h + 1) * DK:(2 * h + 2) * DK].astype(BF16)
        qT_ref[0, h, 0] = jnp.concatenate([q1, zeros, ones_rows], axis=0)
        qT_ref[0, h, 1] = jnp.concatenate([zeros, q2, ones_rows], axis=0)
        ka_ref[0, h, :, 0:2 * DK] = k[:, h * 2 * DK:(h + 1) * 2 * DK].astype(BF16)
        ka_ref[0, h, :, 2 * DK:ATT_K] = kb_ref[h]
        v0 = HEADS * 2 * DK + h * DV
        vT_ref[0, h, 0] = jnp.concatenate([qv[v0:v0 + DV].astype(BF16), sum_rows], axis=0)


def _qkv_proj(x2, wqvT, w_in, kbias, layer, b, s):
    n, d = x2.shape
    tm = QKV_ROWS
    t = ATT_TILE
    per_b = s // tm
    sub = t // tm
    k_width = COL_V - COL_K
    return pl.pallas_call(
        _qkv_kernel,
        grid=(n // tm,),
        in_specs=[pl.BlockSpec((tm, d), lambda i: (i, 0)),
                  _resident(wqvT.shape, layer),
                  _resident((DEPTH, d, k_width), layer, COL_K // k_width),
                  pl.BlockSpec((HEADS, tm, ATT_K - 2 * DK), lambda i: (0, i % sub, 0))],
        out_specs=[
            pl.BlockSpec((1, HEADS, 2, ATT_K, tm), lambda i: (i // per_b, 0, 0, 0, i % per_b)),
            pl.BlockSpec((1, HEADS, tm, ATT_K), lambda i: (i // per_b, 0, i % per_b, 0)),
            pl.BlockSpec((1, HEADS, 1, DV_AUG, tm),
                         lambda i: (i // per_b, 0, (i % per_b) // sub, 0, i % sub)),
        ],
        out_shape=[jax.ShapeDtypeStruct((b, HEADS, 2, ATT_K, s), BF16),
                   jax.ShapeDtypeStruct((b, HEADS, s, ATT_K), BF16),
                   jax.ShapeDtypeStruct((b, HEADS, s // t, DV_AUG, t), BF16)],
        compiler_params=_cparams(("parallel",)),
        name="qkv_proj",
    )(x2, wqvT, w_in, kbias)


def _proj_kernel(x_ref, w_ref, o_ref):
    o_ref[...] = jnp.dot(x_ref[...].astype(BF16), w_ref[...],
                         preferred_element_type=F32).astype(o_ref.dtype)


def _rest_proj(x2, w_in, layer, tm, tn):
    n, d = x2.shape
    gate_blocks = (w_in.shape[2] - COL_GATES) // tn
    first_gate = COL_GATES // tn
    first_rest = COL_REST // tn

    def w_block(i, j):
        return (layer, 0, jnp.where(j < gate_blocks, j + first_gate, j - gate_blocks + first_rest))

    return pl.pallas_call(
        _proj_kernel,
        grid=(n // tm, REST_COLS // tn),
        in_specs=[pl.BlockSpec((tm, d), lambda i, j: (i, 0)),
                  pl.BlockSpec((None, d, tn), w_block)],
        out_specs=pl.BlockSpec((tm, tn), lambda i, j: (i, j)),
        out_shape=jax.ShapeDtypeStruct((n, REST_COLS), F32),
        compiler_params=_cparams(("parallel", "arbitrary")),
        name="in_proj",
    )(x2, w_in)


def _attn_kernel(slope_ref, lam_ref, qT_ref, qTn_ref, ka_ref, vT_ref, dg_ref, w_ref, o_ref,
                 s_scr, cm_scr, m_scr, acc_scr, *, tile, layer, out_scale):
    h = pl.program_id(1)
    qi = pl.program_id(2)
    slope = slope_ref[h]
    lam = lam_ref[layer]

    m_scr[...] = jnp.full(m_scr.shape, NEG_INF, F32)
    acc_scr[...] = jnp.zeros(acc_scr.shape, F32)

    def scores(j, slot, q_ref=qT_ref):
        kt = ka_ref[0, 0, pl.ds(pl.multiple_of(j * tile, tile), tile), :]
        for mp in range(2):
            u = jnp.dot(kt, q_ref[0, 0, mp], preferred_element_type=F32)
            s_scr[slot, mp] = u
            cm_scr[slot, mp] = jnp.max(u, axis=0, keepdims=True)

    def update(mp, u, cm, shift, j):
        m_old = m_scr[mp]
        m_new = jnp.maximum(m_old, cm + shift)
        a = jnp.exp2(m_old - m_new)
        p = jnp.exp2(u - (m_new - shift))
        pv = jnp.dot(vT_ref[0, 0, j], p.astype(BF16), preferred_element_type=F32)
        acc_scr[mp] = a * acc_scr[mp] + pv
        m_scr[mp] = m_new

    @pl.when(qi == 0)
    def _():
        scores(0, 0)

    def past_tile(j, src, dst):
        shift = slope * ((j - qi) * tile).astype(F32)
        us = [s_scr[src, mp] for mp in range(2)]
        cms = [cm_scr[src, mp] for mp in range(2)]
        scores(j + 1, dst)
        for mp in range(2):
            update(mp, us[mp], cms[mp], shift, j)

    def past_single(j, carry):
        past_tile(j, 0, 0)
        return carry

    done = 0
    for size in ATT_GROUPS:
        def past_group(jj, carry, size=size, base=done):
            for r in range(size):
                past_tile(base + size * jj + r, r % 2, (r + 1) % 2)
            return carry

        trips = (qi - done) // size
        lax.fori_loop(0, trips, past_group, 0)
        done = done + trips * size
    lax.fori_loop(done, qi, past_single, 0)

    us = [s_scr[0, mp] + dg_ref[0] for mp in range(2)]
    scores(0, 0, qTn_ref)
    for mp in range(2):
        update(mp, us[mp], jnp.max(us[mp], axis=0, keepdims=True), jnp.float32(0.0), qi)

    o = (acc_scr[0, 0:DV] / acc_scr[0, DV:DV + 1]
         - lam * (acc_scr[1, 0:DV] / acc_scr[1, DV:DV + 1]))
    ms = jnp.mean(o * o, axis=0, keepdims=True)
    o = o * lax.rsqrt(ms + RMS_EPS)
    o = o * w_ref[...] * out_scale
    o_ref[0] = o.T


def _attention(slopes, lam, qT, ka, vT, diag, subln_col, layer, lambda_init):
    b, h, _, _, s = qT.shape
    t = ATT_TILE
    nt = s // t
    kern = functools.partial(_attn_kernel, tile=t, layer=layer, out_scale=1.0 - lambda_init)
    return pl.pallas_call(
        kern,
        grid=(b, h, nt),
        in_specs=[
            pl.BlockSpec(memory_space=pltpu.SMEM),
            pl.BlockSpec(memory_space=pltpu.SMEM),
            pl.BlockSpec((1, 1, 2, ATT_K, t), lambda bi, hi, qi: (bi, hi, 0, 0, qi)),
            pl.BlockSpec((1, 1, 2, ATT_K, t),
                         lambda bi, hi, qi: (bi, hi, 0, 0, jnp.minimum(qi + 1, nt - 1))),
            pl.BlockSpec((1, 1, s, ATT_K), lambda bi, hi, qi: (bi, hi, 0, 0)),
            pl.BlockSpec((1, 1, nt, DV_AUG, t), lambda bi, hi, qi: (bi, hi, 0, 0, 0)),
            pl.BlockSpec((1, t, t), lambda bi, hi, qi: (hi, 0, 0)),
            pl.BlockSpec((None, DV, 1), lambda bi, hi, qi: (layer, 0, 0)),
        ],
        out_specs=pl.BlockSpec((1, t, DV), lambda bi, hi, qi: (bi, qi, hi)),
        out_shape=jax.ShapeDtypeStruct((b, s, ATT_WIDTH), F32),
        scratch_shapes=[pltpu.VMEM((2, 2, t, t), F32),
                        pltpu.VMEM((2, 2, 1, t), F32),
                        pltpu.VMEM((2, 1, t), F32),
                        pltpu.VMEM((2, DV_AUG, t), F32)],
        compiler_params=_cparams(("arbitrary", "arbitrary", "arbitrary")),
        name="diff_attn",
    )(slopes, lam, qT, qT, ka, vT, diag, subln_col)


def _ssm_kernel(u_ref, krow_ref, qre_ref, qim_ref, ptre_ref, ptim_ref, are_ref, aim_ref,
                y_ref, toep_scr, ua_scr, ub_scr, up_scr, yp_scr, zre_scr, zim_scr, xre_scr,
                xim_scr, st_re, st_im, *, ncb):
    npb = SSM_PAIR_BLOCK
    pw = SSM_PAIR_W
    w = SSM_ROW_W
    rows8 = 8 * SSM_T
    halves = (ua_scr, ub_scr)
    per_half = LANES // pw
    nt_dims = (((1,), (1,)), ((), ()))

    @pl.when(pl.program_id(2) == 0)
    def _():
        st_re[...] = jnp.zeros(st_re.shape, F32)
        st_im[...] = jnp.zeros(st_im.shape, F32)
        lane = lax.broadcasted_iota(jnp.int32, (pw, w), 1)
        for pi in range(npb):
            kr = krow_ref[pi]
            for tau in range(SSM_T):
                blk = kr if tau == 0 else jnp.where(
                    lane >= tau * pw, pltpu.roll(kr, tau * pw, axis=1), 0.0)
                toep_scr[pi, tau * pw:(tau + 1) * pw, :] = blk.astype(BF16)

    ua_scr[...] = u_ref[:, 0:LANES]
    ub_scr[...] = u_ref[:, LANES:2 * LANES]

    def gather(rb, carry):
        tok = pl.multiple_of(rb * rows8, rows8)
        row = pl.multiple_of(rb * 8, 8)
        for tau in range(SSM_T):
            for hf in range(2):
                ut = halves[hf][pl.ds(tok + tau, 8, stride=SSM_T), :]
                for q in range(per_half):
                    up_scr[hf * per_half + q, pl.ds(row, 8), tau * pw:(tau + 1) * pw] = (
                        ut[:, q * pw:(q + 1) * pw])
        return carry

    lax.fori_loop(0, ncb // 8, gather, 0, unroll=SSM_RELAYOUT_UNROLL)

    for pi in range(npb):
        up = up_scr[pi].astype(BF16)
        zre_scr[pl.ds(pi, ncb, stride=npb), :] = jnp.dot(
            up, qre_ref[:, pi].reshape(w, SSM_STATE_W), preferred_element_type=F32)
        zim_scr[pl.ds(pi, ncb, stride=npb), :] = jnp.dot(
            up, qim_ref[:, pi].reshape(w, SSM_STATE_W), preferred_element_type=F32)

    ar = are_ref[...]
    ai = aim_ref[...]

    def step(c, carry):
        xr, xi = carry
        row = pl.multiple_of(c * npb, npb)
        xre_scr[pl.ds(row, npb), :] = xr
        xim_scr[pl.ds(row, npb), :] = xi
        nxr = ar * xr - ai * xi + zre_scr[pl.ds(row, npb), :]
        nxi = ar * xi + ai * xr + zim_scr[pl.ds(row, npb), :]
        return nxr, nxi

    xr, xi = lax.fori_loop(0, ncb, step, (st_re[...], st_im[...]))
    st_re[...] = xr
    st_im[...] = xi

    for pi in range(npb):
        up = up_scr[pi].astype(BF16)
        xrp = xre_scr[pl.ds(pi, ncb, stride=npb), :].astype(BF16)
        xip = xim_scr[pl.ds(pi, ncb, stride=npb), :].astype(BF16)
        yp_scr[pi] = (
            jnp.dot(up, toep_scr[pi], preferred_element_type=F32)
            + lax.dot_general(xrp, ptre_ref[:, pi].reshape(w, SSM_STATE_W), nt_dims,
                              preferred_element_type=F32)
            + lax.dot_general(xip, ptim_ref[:, pi].reshape(w, SSM_STATE_W), nt_dims,
                              preferred_element_type=F32))

    def scatter(rb, carry):
        tok = pl.multiple_of(rb * rows8, rows8)
        row = pl.multiple_of(rb * 8, 8)
        for tau in range(SSM_T):
            for hf in range(2):
                yt = jnp.concatenate(
                    [yp_scr[hf * per_half + q, pl.ds(row, 8), tau * pw:(tau + 1) * pw]
                     for q in range(per_half)], axis=1)
                halves[hf][pl.ds(tok + tau, 8, stride=SSM_T), :] = yt
        return carry

    lax.fori_loop(0, ncb // 8, scatter, 0, unroll=SSM_RELAYOUT_UNROLL)
    y_ref[:, 0:LANES] = ua_scr[...]
    y_ref[:, LANES:2 * LANES] = ub_scr[...]


def _ssm_scan(rest, layer, b, s, krow, qre, qim, ptre, ptim, are, aim):
    n = rest.shape[0]
    npb = SSM_PAIR_BLOCK
    ncb = min(SSM_CHUNKS, s // SSM_T)
    tok = ncb * SSM_T
    cw = npb * SSM_PAIR_W
    nblk = s // tok
    u_col0 = R_U // cw
    w = SSM_ROW_W
    kern = functools.partial(_ssm_kernel, ncb=ncb)
    return pl.pallas_call(
        kern,
        grid=(b, SSM_PAIRS // npb, nblk),
        in_specs=[
            pl.BlockSpec((tok, cw), lambda bi, gi, ti: (bi * nblk + ti, u_col0 + gi)),
            pl.BlockSpec((None, npb, SSM_PAIR_W, w), lambda bi, gi, ti: (layer, gi, 0, 0)),
        ] + [pl.BlockSpec((None, SSM_T, npb, SSM_PAIR_W, SSM_STATE_W),
                          lambda bi, gi, ti: (layer, 0, gi, 0, 0)) for _ in range(4)] + [
            pl.BlockSpec((None, npb, SSM_STATE_W), lambda bi, gi, ti: (layer, gi, 0)),
            pl.BlockSpec((None, npb, SSM_STATE_W), lambda bi, gi, ti: (layer, gi, 0)),
        ],
        out_specs=pl.BlockSpec((tok, cw), lambda bi, gi, ti: (bi * nblk + ti, gi)),
        out_shape=jax.ShapeDtypeStruct((n, SSM_WIDTH), F32),
        scratch_shapes=[pltpu.VMEM((npb, w, w), BF16),
                        pltpu.VMEM((tok, LANES), F32), pltpu.VMEM((tok, LANES), F32),
                        pltpu.VMEM((npb, ncb, w), F32), pltpu.VMEM((npb, ncb, w), F32)]
        + [pltpu.VMEM((ncb * npb, SSM_STATE_W), F32) for _ in range(4)]
        + [pltpu.VMEM((npb, SSM_STATE_W), F32), pltpu.VMEM((npb, SSM_STATE_W), F32)],
        compiler_params=_cparams(("parallel", "parallel", "arbitrary")),
        name="s5_scan",
    )(rest, krow, qre, qim, ptre, ptim, are, aim)


def _ssm_matrices(a_re, a_im, log_dt, b_re, b_im, c_re, c_im):
    g, p, t, hh = SSM_GROUPS, SSM_STATE, SSM_T, SSM_GROUP
    ar, ai = a_re.astype(F32), a_im.astype(F32)
    dt = jnp.exp(log_dt.astype(F32))[:, None]
    mag = jnp.exp(ar * dt)
    ph = ai * dt
    abr, abi = mag * jnp.cos(ph), mag * jnp.sin(ph)
    nr, ni = abr - 1.0, abi
    den = ar * ar + ai * ai
    fr = (nr * ar + ni * ai) / den
    fi = (ni * ar - nr * ai) / den
    br, bi = b_re.astype(F32), b_im.astype(F32)
    bbr = fr[..., None] * br - fi[..., None] * bi
    bbi = fr[..., None] * bi + fi[..., None] * br
    cr, ci = c_re.astype(F32), c_im.astype(F32)

    jj = jnp.arange(t + 1, dtype=F32)[:, None, None]
    mag_j = jnp.exp(jj * (ar * dt)[None])
    pwr, pwi = mag_j * jnp.cos(jj * ph[None]), mag_j * jnp.sin(jj * ph[None])

    cpr = cr[None] * pwr[:t, :, None, :] - ci[None] * pwi[:t, :, None, :]
    cpi = cr[None] * pwi[:t, :, None, :] + ci[None] * pwr[:t, :, None, :]
    cpc = jnp.concatenate([cpr, -cpi], axis=-1)
    bbc = jnp.concatenate([bbr, bbi], axis=1)
    kj = jnp.sum(cpc[..., None] * bbc[None, :, None, :, :], axis=3)
    eye2 = jnp.eye(2, dtype=F32)
    kj6 = kj.reshape(t, SSM_PAIRS, 2, hh, hh)
    krow = jnp.einsum('jpgyx,gk->pgxjky', kj6, eye2).reshape(SSM_PAIRS, SSM_PAIR_W, SSM_ROW_W)

    rr, ri = pwr[t - 1::-1][:t], pwi[t - 1::-1][:t]
    bbr_t, bbi_t = bbr.transpose(0, 2, 1), bbi.transpose(0, 2, 1)
    qr = rr[:, :, None, :] * bbr_t[None] - ri[:, :, None, :] * bbi_t[None]
    qi = rr[:, :, None, :] * bbi_t[None] + ri[:, :, None, :] * bbr_t[None]

    wr = cr[None] * pwr[1:, :, None, :] - ci[None] * pwi[1:, :, None, :]
    wi = cr[None] * pwi[1:, :, None, :] + ci[None] * pwr[1:, :, None, :]

    def pair_block_diag(m):
        m = m.reshape(t, SSM_PAIRS, 2, hh, p)
        first = (jnp.arange(2) == 0)[None, None, :, None, None]
        wide = jnp.concatenate([jnp.where(first, m, 0.0), jnp.where(first, 0.0, m)], axis=-1)
        return wide.astype(BF16).reshape(t, SSM_PAIRS, SSM_PAIR_W, 2 * p)

    are = pwr[t].reshape(SSM_PAIRS, 2 * p)
    aim = pwi[t].reshape(SSM_PAIRS, 2 * p)
    return (krow, pair_block_diag(qr), pair_block_diag(qi),
            pair_block_diag(wr), pair_block_diag(-wi), are, aim)


def _merge_kernel(ya_ref, za_ref, ys_ref, u_ref, zs_ref, ga_ref, gs_ref,
                  d_ref, wglu_ref, wao_ref, wso_ref, o_ref):
    ya = ya_ref[...] * jax.nn.silu(za_ref[...])
    y = ys_ref[...] + d_ref[...] * u_ref[...]
    y = jax.nn.gelu(y)
    y = y * jax.nn.sigmoid(jnp.dot(y.astype(BF16), wglu_ref[...], preferred_element_type=F32))
    ys = y * jax.nn.silu(zs_ref[...])
    oa = jnp.dot(ya.astype(BF16), wao_ref[...], preferred_element_type=F32)
    os_ = jnp.dot(ys.astype(BF16), wso_ref[...], preferred_element_type=F32)
    merged = jax.nn.sigmoid(ga_ref[...]) * oa + jax.nn.sigmoid(gs_ref[...]) * os_
    o_ref[...] = merged.astype(o_ref.dtype)


def _merge(ya, ys, rest, d_row, wglu, wao, wso, layer, tm):
    n = ya.shape[0]
    d = wao.shape[2]
    w1 = ATT_WIDTH
    cb = lambda off, wd: pl.BlockSpec((tm, wd), lambda i: (i, off // wd))
    return pl.pallas_call(
        _merge_kernel,
        grid=(n // tm,),
        in_specs=[cb(0, w1), cb(R_ZATT, w1), cb(0, w1), cb(R_U, w1), cb(R_ZSSM, w1),
                  cb(R_GATT, d), cb(R_GSSM, d),
                  _resident(d_row.shape, layer), _resident(wglu.shape, layer),
                  _resident(wao.shape, layer), _resident(wso.shape, layer)],
        out_specs=pl.BlockSpec((tm, d), lambda i: (i, 0)),
        out_shape=jax.ShapeDtypeStruct((n, d), BF16),
        compiler_params=_cparams(("parallel",)),
        name="branch_merge",
    )(ya, rest, ys, rest, rest, rest, rest, d_row, wglu, wao, wso)


def _out_kernel(mg_ref, gp_ref, p_ref, x_ref, wout_ref, wple_ref, g_ref, b_ref, o_ref):
    mo = jnp.dot(mg_ref[...], wout_ref[...], preferred_element_type=F32)
    ple = jax.nn.sigmoid(gp_ref[...]) * jnp.dot(
        p_ref[...].astype(BF16), wple_ref[...], preferred_element_type=F32)
    r = ALPHA * x_ref[...] + mo + ple
    mu = jnp.mean(r, axis=-1, keepdims=True)
    rc = r - mu
    var = jnp.mean(rc * rc, axis=-1, keepdims=True)
    o_ref[...] = rc * lax.rsqrt(var + LN_EPS) * g_ref[...] + b_ref[...]


def _out_ln(merged, rest, p3, x2, wout, wple, g_row, b_row, layer, tm):
    n, d = x2.shape
    return pl.pallas_call(
        _out_kernel,
        grid=(n // tm,),
        in_specs=[pl.BlockSpec((tm, d), lambda i: (i, 0)),
                  pl.BlockSpec((tm, d), lambda i: (i, R_GPLE // d)),
                  pl.BlockSpec((None, tm, p3.shape[2]), lambda i: (layer, i, 0)),
                  pl.BlockSpec((tm, d), lambda i: (i, 0)),
                  _resident(wout.shape, layer), _resident(wple.shape, layer),
                  _resident(g_row.shape, layer), _resident(b_row.shape, layer)],
        out_specs=pl.BlockSpec((tm, d), lambda i: (i, 0)),
        out_shape=jax.ShapeDtypeStruct((n, d), F32),
        compiler_params=_cparams(("parallel",)),
        name="out_ln",
    )(merged, rest, p3, x2, wout, wple, g_row, b_row)


def _key_bias_columns(slopes):
    t = ATT_TILE
    v = slopes[:, None] * jnp.arange(t, dtype=F32)[None, :]
    parts = []
    for _ in range(BIAS_PARTS):
        top = lax.bitcast_convert_type(
            lax.bitcast_convert_type(v, jnp.uint32) & jnp.uint32(0xFFFF0000), F32)
        parts.append(top)
        v = v - top
    cols = jnp.stack(parts, axis=-1)
    pad = jnp.zeros((HEADS, t, ATT_K - 2 * DK - BIAS_PARTS), F32)
    return jnp.concatenate([cols, pad], axis=-1).astype(BF16)


def _diag_tile_terms(slopes):
    t = ATT_TILE
    ik = jnp.arange(t, dtype=jnp.int32)[:, None]
    iq = jnp.arange(t, dtype=jnp.int32)[None, :]
    ahead = jnp.maximum(ik - iq, 0).astype(F32)
    allowed = (ik // CHUNK) <= (iq // CHUNK)
    return jnp.where(allowed[None], -2.0 * slopes[:, None, None] * ahead[None], NEG_INF)


def kernel(x, p, w_in, w_att_out, w_ssm_out, w_out, w_ple, lambda_q1, lambda_k1, lambda_q2,
           lambda_k2, subln_w, ssm_a_re, ssm_a_im, ssm_log_dt, ssm_b_re, ssm_b_im, ssm_c_re,
           ssm_c_im, ssm_d, ssm_w_glu, ln_g, ln_b):
    b, s, d = x.shape
    n = b * s
    x2 = x.reshape(n, d)
    p3 = p.reshape(DEPTH, n, PLE_DIM)

    lambda_inits = [0.8 - 0.6 * math.exp(-0.3 * i) for i in range(DEPTH)]
    lam = (jnp.exp(jnp.sum(lambda_q1.astype(F32) * lambda_k1.astype(F32), axis=-1))
           - jnp.exp(jnp.sum(lambda_q2.astype(F32) * lambda_k2.astype(F32), axis=-1))
           + jnp.asarray(lambda_inits, F32))
    slopes = LOG2E * 2.0 ** (-8.0 * (jnp.arange(HEADS, dtype=F32) + 1.0) / HEADS)
    kbias = _key_bias_columns(slopes)
    diag = _diag_tile_terms(slopes)
    scale = LOG2E / math.sqrt(DK)
    wqvT = _wqv_transposed(w_in.astype(F32), scale)
    w_in_b = w_in.astype(BF16)
    w_glu_b, w_ao_b, w_so_b = (ssm_w_glu.astype(BF16), w_att_out.astype(BF16),
                               w_ssm_out.astype(BF16))
    w_out_b, w_ple_b = w_out.astype(BF16), w_ple.astype(BF16)
    subln_col = subln_w.astype(F32).reshape(DEPTH, DV, 1)
    d_row = ssm_d.astype(F32).reshape(DEPTH, 1, SSM_WIDTH)
    g_row = ln_g.astype(F32).reshape(DEPTH, 1, d)
    b_row = ln_b.astype(F32).reshape(DEPTH, 1, d)
    mats = jax.vmap(_ssm_matrices)(ssm_a_re, ssm_a_im, ssm_log_dt, ssm_b_re, ssm_b_im,
                                   ssm_c_re, ssm_c_im)

    for i in range(DEPTH):
        qT, ka, vT = _qkv_proj(x2, wqvT, w_in_b, kbias, i, b, s)
        rest = _rest_proj(x2, w_in_b, i, 1024, 1024)
        y_att = _attention(slopes, lam, qT, ka, vT, diag, subln_col, i, lambda_inits[i])
        ys = _ssm_scan(rest, i, b, s, *mats)
        merged = _merge(y_att.reshape(n, ATT_WIDTH), ys, rest, d_row, w_glu_b, w_ao_b, w_so_b,
                        i, 256)
        x2 = _out_ln(merged, rest, p3, x2, w_out_b, w_ple_b, g_row, b_row, i, 256)
    return x2.reshape(b, s, d)
```
